```python
import math
import jax, jax.numpy as jnp
from jax import lax
import numpy as np

D_MODEL = 1024
BATCH = 2
SEQ = 16384
DEPTH = 2

HEAD_DIM = 64
CONV_CH = 256
CONV_GROUPS = CONV_CH // HEAD_DIM
CONV_WIDTH = 31
ATTN_HEADS = 8
ATTN_CH = ATTN_HEADS * HEAD_DIM
DILATED_PATTERNS = ((128, 1), (512, 4), (2048, 16))
ATTN_BLOCK = 128
N_BUCKETS = 32
MAX_DISTANCE = 2048
GMLP_CH = 256
GMLP_GROUPS = 4
GMLP_GROUP_DIM = GMLP_CH // GMLP_GROUPS
CHUNK = 128
MIX_CH = CONV_CH + ATTN_CH + GMLP_CH
IN_CH = 2 * CONV_CH + 3 * ATTN_CH + 2 * GMLP_CH
D_FF = 2816
FFN_CONV_WIDTH = 3
LN_EPS = 1e-5
ALPHA = (2.0 * DEPTH) ** 0.25
BETA = (8.0 * DEPTH) ** -0.25

kernel_name = "hymba_style_conv_dilattn_gmlp_convffn_deepnorm"


def _layernorm(x, g, b):
    xf = x.astype(jnp.float32)
    mu = jnp.mean(xf, axis=-1, keepdims=True)
    var = jnp.mean(jnp.square(xf - mu), axis=-1, keepdims=True)
    y = (xf - mu) * lax.rsqrt(var + LN_EPS)
    return (y * g.astype(jnp.float32) + b.astype(jnp.float32)).astype(x.dtype)


def _causal_dwconv(x, w, b):
    K, C = w.shape
    y = lax.conv_general_dilated(x, w[:, None, :].astype(x.dtype), window_strides=(1,),
                                 padding=((K - 1, 0),), dimension_numbers=("NWC", "WIO", "NWC"),
                                 feature_group_count=C)
    return y + b.astype(x.dtype)


def _t5_bucket(dist):
    max_exact = N_BUCKETS // 2
    d = np.maximum(dist, 1).astype(np.float64)
    large = max_exact + (np.log(d / max_exact) / math.log(MAX_DISTANCE / max_exact)
                         * (N_BUCKETS - max_exact)).astype(np.int32)
    large = np.minimum(large, N_BUCKETS - 1)
    return np.where(dist < max_exact, dist, large).astype(np.int32)


def _conv_module(a_in, dw_w, dw_b, ln_g, ln_b):
    a, gate = jnp.split(a_in, 2, axis=-1)
    h = a * jax.nn.sigmoid(gate)
    h = _causal_dwconv(h, dw_w, dw_b)
    h = _layernorm(h, ln_g, ln_b)
    return jax.nn.silu(h)


def _dilated_branch(q, k, v, rel_table, window, dilation):
    B, S, H, Dh = q.shape
    L = S // dilation
    n_win = window // dilation
    nb = -(-L // ATTN_BLOCK)
    Lp = nb * ATTN_BLOCK

    def to_blocks(t):
        t = t.reshape(B, L, dilation, H, Dh).transpose(0, 2, 1, 3, 4)
        t = jnp.pad(t, ((0, 0), (0, 0), (0, Lp - L), (0, 0), (0, 0)))
        return t.reshape(B, dilation, nb, ATTN_BLOCK, H, Dh)

    def with_prev(t):
        prev = jnp.pad(t, ((0, 0), (0, 0), (1, 0), (0, 0), (0, 0), (0, 0)))[:, :, :-1]
        return jnp.concatenate([prev, t], axis=3)

    qb = to_blocks(q)
    kk = with_prev(to_blocks(k))
    vv = with_prev(to_blocks(v))

    qi = np.arange(ATTN_BLOCK)[:, None]
    kj = np.arange(2 * ATTN_BLOCK)[None, :]
    dist_sub = qi + ATTN_BLOCK - kj
    kpos = np.arange(nb)[:, None, None] * ATTN_BLOCK + kj[None] - ATTN_BLOCK
    valid = (dist_sub >= 0)[None] & (dist_sub <= n_win)[None] & (kpos >= 0)
    bucket = _t5_bucket(np.clip(dist_sub, 0, None) * dilation)
    bias = jnp.transpose(rel_table[bucket].astype(jnp.float32), (2, 0, 1))

    logits = jnp.einsum("brnqhd,brnkhd->brnhqk", qb, kk).astype(jnp.float32) + bias
    logits = jnp.where(jnp.asarray(valid)[None, None, :, None], logits, -jnp.inf)
    m = jnp.max(logits, axis=-1)
    p = jnp.exp(logits - m[..., None])
    s = jnp.sum(p, axis=-1)
    o = jnp.einsum("brnhqk,brnkhd->brnqhd", p, vv.astype(jnp.float32))

    def from_blocks(t):
        rest = t.shape[4:]
        t = t.reshape((B, dilation, Lp) + rest)[:, :, :L]
        t = jnp.moveaxis(t, 1, 2)
        return t.reshape((B, S) + rest)

    m = from_blocks(jnp.swapaxes(m, 3, 4))
    s = from_blocks(jnp.swapaxes(s, 3, 4))
    o = from_blocks(o)
    return m, s, o


def _dilated_attention(q, k, v, rel_table):
    q = q * (HEAD_DIM ** -0.5)
    outs = [_dilated_branch(q, k, v, rel_table, w, d) for (w, d) in DILATED_PATTERNS]
    ms = jnp.stack([r[0] for r in outs])
    big_m = jnp.max(ms, axis=0)
    wts = jnp.exp(ms - big_m)
    den = sum(wts[i] * outs[i][1] for i in range(len(outs)))
    num = sum(wts[i][..., None] * outs[i][2] for i in range(len(outs)))
    return (num / den[..., None]).astype(q.dtype)


def _spatial_gating(c_in, ln_g, ln_b, w_s, b_s):
    B, S, _ = c_in.shape
    u, v = jnp.split(c_in, 2, axis=-1)
    v = _layernorm(v, ln_g, ln_b)
    vc = v.reshape(B, S // CHUNK, CHUNK, GMLP_GROUPS, GMLP_GROUP_DIM)
    w = jnp.tril(w_s).astype(v.dtype)
    mixed = jnp.einsum("gts,bcsgd->bctgd", w, vc) + b_s.T.astype(v.dtype)[None, None, :, :, None]
    return u * mixed.reshape(B, S, GMLP_CH)


def _conv_ffn(x, w_up, b_up, conv_w, conv_b, w_down, b_down):
    h = x @ w_up + b_up
    h = _causal_dwconv(h, conv_w, conv_b)
    g, val = jnp.split(h, 2, axis=-1)
    return (jax.nn.silu(g) * val) @ w_down + b_down


def setup_inputs(seed: int = 0) -> dict:
    key = jax.random.key(seed)
    ks = jax.random.split(key, 24)
    f32 = jnp.float32

    def nrm(k, shape, scale):
        return jax.random.normal(k, shape, f32) * scale

    x = nrm(ks[0], (BATCH, SEQ, D_MODEL), 1.0)
    w_in = nrm(ks[1], (DEPTH, D_MODEL, IN_CH), D_MODEL ** -0.5)
    v_lo = 2 * CONV_CH + 2 * ATTN_CH
    v_scale = jnp.ones((IN_CH,), f32).at[v_lo:v_lo + ATTN_CH].set(BETA)
    w_in = w_in * v_scale
    b_in = nrm(ks[2], (DEPTH, IN_CH), 0.02)
    conv_dw_w = nrm(ks[3], (DEPTH, CONV_WIDTH, CONV_CH), CONV_WIDTH ** -0.5)
    conv_dw_b = nrm(ks[4], (DEPTH, CONV_CH), 0.02)
    conv_ln_g = 1.0 + nrm(ks[5], (DEPTH, CONV_CH), 0.02)
    conv_ln_b = nrm(ks[6], (DEPTH, CONV_CH), 0.02)
    rel_bias_table = nrm(ks[7], (N_BUCKETS, ATTN_HEADS), 0.5)
    gmlp_ln_g = 1.0 + nrm(ks[8], (DEPTH, GMLP_CH), 0.02)
    gmlp_ln_b = nrm(ks[9], (DEPTH, GMLP_CH), 0.02)
    gmlp_w_s = nrm(ks[10], (DEPTH, GMLP_GROUPS, CHUNK, CHUNK), CHUNK ** -0.5)
    gmlp_b_s = 1.0 + nrm(ks[11], (DEPTH, GMLP_GROUPS, CHUNK), 0.02)
    w_out = nrm(ks[12], (DEPTH, MIX_CH, D_MODEL), MIX_CH ** -0.5 * BETA)
    b_out = nrm(ks[13], (DEPTH, D_MODEL), 0.02)
    ln1_g = 1.0 + nrm(ks[14], (DEPTH, D_MODEL), 0.02)
    ln1_b = nrm(ks[15], (DEPTH, D_MODEL), 0.02)
    ffn_w_up = nrm(ks[16], (DEPTH, D_MODEL, 2 * D_FF), D_MODEL ** -0.5)
    ffn_b_up = nrm(ks[17], (DEPTH, 2 * D_FF), 0.02)
    ffn_conv_w = nrm(ks[18], (DEPTH, FFN_CONV_WIDTH, 2 * D_FF), FFN_CONV_WIDTH ** -0.5)
    ffn_conv_b = nrm(ks[19], (DEPTH, 2 * D_FF), 0.02)
    ffn_w_down = nrm(ks[20], (DEPTH, D_FF, D_MODEL), D_FF ** -0.5 * BETA)
    ffn_b_down = nrm(ks[21], (DEPTH, D_MODEL), 0.02)
    ln2_g = 1.0 + nrm(ks[22], (DEPTH, D_MODEL), 0.02)
    ln2_b = nrm(ks[23], (DEPTH, D_MODEL), 0.02)
    return {"x": x, "w_in": w_in, "b_in": b_in, "conv_dw_w": conv_dw_w, "conv_dw_b": conv_dw_b,
            "conv_ln_g": conv_ln_g, "conv_ln_b": conv_ln_b, "rel_bias_table": rel_bias_table,
            "gmlp_ln_g": gmlp_ln_g, "gmlp_ln_b": gmlp_ln_b, "gmlp_w_s": gmlp_w_s, "gmlp_b_s": gmlp_b_s,
            "w_out": w_out, "b_out": b_out, "ln1_g": ln1_g, "ln1_b": ln1_b,
            "ffn_w_up": ffn_w_up, "ffn_b_up": ffn_b_up, "ffn_conv_w": ffn_conv_w, "ffn_conv_b": ffn_conv_b,
            "ffn_w_down": ffn_w_down, "ffn_b_down": ffn_b_down, "ln2_g": ln2_g, "ln2_b": ln2_b}


def reference(x, w_in, b_in, conv_dw_w, conv_dw_b, conv_ln_g, conv_ln_b, rel_bias_table,
              gmlp_ln_g, gmlp_ln_b, gmlp_w_s, gmlp_b_s, w_out, b_out, ln1_g, ln1_b,
              ffn_w_up, ffn_b_up, ffn_conv_w, ffn_conv_b, ffn_w_down, ffn_b_down, ln2_g, ln2_b):
    B, S, _ = x.shape
    split_pts = [2 * CONV_CH, 2 * CONV_CH + ATTN_CH, 2 * CONV_CH + 2 * ATTN_CH,
                 2 * CONV_CH + 3 * ATTN_CH]
    for l in range(DEPTH):
        h = x @ w_in[l] + b_in[l]
        a_in, q, k, v, c_in = jnp.split(h, split_pts, axis=-1)
        conv_out = _conv_module(a_in, conv_dw_w[l], conv_dw_b[l], conv_ln_g[l], conv_ln_b[l])
        attn_out = _dilated_attention(q.reshape(B, S, ATTN_HEADS, HEAD_DIM),
                                      k.reshape(B, S, ATTN_HEADS, HEAD_DIM),
                                      v.reshape(B, S, ATTN_HEADS, HEAD_DIM),
                                      rel_bias_table).reshape(B, S, ATTN_CH)
        gm_out = _spatial_gating(c_in, gmlp_ln_g[l], gmlp_ln_b[l], gmlp_w_s[l], gmlp_b_s[l])
        mix = jnp.concatenate([conv_out, attn_out, gm_out], axis=-1) @ w_out[l] + b_out[l]
        x = _layernorm(ALPHA * x + mix, ln1_g[l], ln1_b[l])
        ffn = _conv_ffn(x, ffn_w_up[l], ffn_b_up[l], ffn_conv_w[l], ffn_conv_b[l],
                        ffn_w_down[l], ffn_b_down[l])
        x = _layernorm(ALPHA * x + ffn, ln2_g[l], ln2_b[l])
    return x
```

```python
import functools
import math

import jax
import jax.numpy as jnp
import numpy as np
from jax import lax
from jax.experimental import pallas as pl
from jax.experimental.pallas import tpu as pltpu

D_MODEL = 1024
HEAD_DIM = 64
CONV_CH = 256
CONV_WIDTH = 31
ATTN_HEADS = 8
ATTN_CH = ATTN_HEADS * HEAD_DIM
DILATED_PATTERNS = ((128, 1), (512, 4), (2048, 16))
ATTN_BLOCK = 128
N_BUCKETS = 32
MAX_DISTANCE = 2048
GMLP_CH = 256
GMLP_GROUPS = 4
GMLP_GROUP_DIM = GMLP_CH // GMLP_GROUPS
CHUNK = 128
MIX_CH = CONV_CH + ATTN_CH + GMLP_CH
IN_CH = 2 * CONV_CH + 3 * ATTN_CH + 2 * GMLP_CH
D_FF = 2816
FFN_CONV_WIDTH = 3
LN_EPS = 1e-5

COL_BLOCK = 512
H_BLOCKS = IN_CH // COL_BLOCK
COL_A, COL_Q, COL_K, COL_V, COL_C = range(5)

ROW_TILE = 512
CONV_HALO = 32
CONV_ROW_CHUNK = 64
ATTN_Q_TILE = 512
FF_CHUNK = 256
NEG_LOGIT = -1e30
VMEM_LIMIT_BYTES = 56 * 1024 * 1024

BF16 = jnp.bfloat16
F32 = jnp.float32


def _const_spec(shape):
    zeros = (0,) * len(shape)
    return pl.BlockSpec(shape, lambda *_: zeros, pipeline_mode=pl.Buffered(1))


def _params(*semantics):
    return pltpu.CompilerParams(dimension_semantics=semantics,
                                vmem_limit_bytes=VMEM_LIMIT_BYTES)


def _layernorm(x, g, b):
    mu = jnp.mean(x, axis=-1, keepdims=True)
    xc = x - mu
    var = jnp.mean(xc * xc, axis=-1, keepdims=True)
    return xc * lax.rsqrt(var + LN_EPS) * g + b


def _sigmoid(x):
    return 1.0 / (1.0 + jnp.exp(-x))


def _inproj_body(x_ref, w_ref, b_ref, h_ref):
    xb = x_ref[...].astype(BF16)
    for j in range(H_BLOCKS):
        cols = slice(j * COL_BLOCK, (j + 1) * COL_BLOCK)
        h_ref[:, cols] = (jnp.dot(xb, w_ref[:, cols], preferred_element_type=F32)
                          + b_ref[:, cols])


def _inproj(x, w, b):
    n = x.shape[0]
    return pl.pallas_call(
        _inproj_body,
        grid=(n // ROW_TILE,),
        in_specs=[pl.BlockSpec((ROW_TILE, D_MODEL), lambda i: (i, 0)),
                  _const_spec((D_MODEL, IN_CH)),
                  _const_spec((1, IN_CH))],
        out_specs=pl.BlockSpec((ROW_TILE, IN_CH), lambda i: (i, 0)),
        out_shape=jax.ShapeDtypeStruct((n, IN_CH), F32),
        compiler_params=_params("parallel"),
        name="inproj",
    )(x, w, b)


def _conv_gmlp_body(tiles_per_seq, a_ref, halo_ref, c_ref, dww_ref, dwb_ref, cg_ref, cb_ref,
                    gg_ref, gb_ref, ws_ref, bs_ref, o_ref, buf_ref):
    first = (pl.program_id(0) % tiles_per_seq) == 0

    halo = halo_ref[...]
    glu_halo = halo[:, :CONV_CH] * _sigmoid(halo[:, CONV_CH:])
    buf_ref[0:CONV_HALO, :] = jnp.where(first, 0.0, glu_halo)
    a = a_ref[...]
    buf_ref[CONV_HALO:, :] = a[:, :CONV_CH] * _sigmoid(a[:, CONV_CH:])

    lead = CONV_HALO - (CONV_WIDTH - 1)

    for r0 in range(0, ROW_TILE, CONV_ROW_CHUNK):
        acc = jnp.broadcast_to(dwb_ref[...], (CONV_ROW_CHUNK, CONV_CH))
        for k in range(CONV_WIDTH):
            start = r0 + lead + k
            acc = acc + dww_ref[k:k + 1, :] * buf_ref[start:start + CONV_ROW_CHUNK, :]
        y = _layernorm(acc, cg_ref[...], cb_ref[...])
        o_ref[r0:r0 + CONV_ROW_CHUNK, 0:CONV_CH] = (y * _sigmoid(y)).astype(BF16)

    lane = lax.broadcasted_iota(jnp.int32, (1, GMLP_CH), 1)
    for ch in range(ROW_TILE // CHUNK):
        rows = slice(ch * CHUNK, (ch + 1) * CHUNK)
        c = c_ref[rows, :]
        vn = _layernorm(c[:, GMLP_CH:], gg_ref[...], gb_ref[...])
        stacked = jnp.concatenate(
            [jnp.where((lane // GMLP_GROUP_DIM) == g, vn, 0.0) for g in range(GMLP_GROUPS)],
            axis=0).astype(BF16)
        mixed = jnp.dot(ws_ref[...], stacked, preferred_element_type=F32) + bs_ref[...]
        o_ref[rows, CONV_CH:] = (c[:, :GMLP_CH] * mixed).astype(BF16)


def _conv_gmlp(h, seq, dww, dwb, cg, cb, gg, gb, ws_cat, bs_tab):
    n = h.shape[0]
    tiles_per_seq = seq // ROW_TILE
    halo_per_tile = ROW_TILE // CONV_HALO
    return pl.pallas_call(
        functools.partial(_conv_gmlp_body, tiles_per_seq),
        grid=(n // ROW_TILE,),
        in_specs=[pl.BlockSpec((ROW_TILE, COL_BLOCK), lambda i: (i, COL_A)),
                  pl.BlockSpec((CONV_HALO, COL_BLOCK),
                               lambda i: (jnp.maximum(i * halo_per_tile - 1, 0), COL_A)),
                  pl.BlockSpec((ROW_TILE, COL_BLOCK), lambda i: (i, COL_C)),
                  _const_spec(dww.shape), _const_spec(dwb.shape),
                  _const_spec(cg.shape), _const_spec(cb.shape),
                  _const_spec(gg.shape), _const_spec(gb.shape),
                  _const_spec(ws_cat.shape), _const_spec(bs_tab.shape)],
        out_specs=pl.BlockSpec((ROW_TILE, CONV_CH + GMLP_CH), lambda i: (i, 0)),
        out_shape=jax.ShapeDtypeStruct((n, CONV_CH + GMLP_CH), BF16),
        scratch_shapes=[pltpu.VMEM((CONV_HALO + ROW_TILE, CONV_CH), F32)],
        compiler_params=_params("parallel"),
        name="conv_gmlp",
    )(h, h, h, dww, dwb, cg, cb, gg, gb, ws_cat, bs_tab)


def _attn_body(q_ref, kc_ref, kp_ref, vc_ref, vp_ref, bias_ref, o_ref, lse_ref, k_buf, v_buf):
    first_tile = pl.program_id(2) == 0
    k_buf[0:ATTN_BLOCK, :] = kp_ref[...].astype(BF16)
    k_buf[ATTN_BLOCK:, :] = kc_ref[...].astype(BF16)
    v_buf[0:ATTN_BLOCK, :] = vp_ref[...].astype(BF16)
    v_buf[ATTN_BLOCK:, :] = vc_ref[...].astype(BF16)

    pair = 2 * HEAD_DIM
    lo = lax.broadcasted_iota(jnp.int32, (1, pair), 1) < HEAD_DIM
    nt = (((1,), (1,)), ((), ()))

    def block(j, carry):
        r0 = pl.multiple_of(j * ATTN_BLOCK, ATTN_BLOCK)
        bias_sel = jnp.where(jnp.logical_and(first_tile, j == 0), 0, 1)
        for hp in range(ATTN_HEADS // 2):
            cols = slice(hp * pair, (hp + 1) * pair)
            q2 = q_ref[pl.ds(r0, ATTN_BLOCK), cols] * (HEAD_DIM ** -0.5)
            k2 = k_buf[pl.ds(r0, 2 * ATTN_BLOCK), cols]
            v2 = v_buf[pl.ds(r0, 2 * ATTN_BLOCK), cols]
            zero = jnp.zeros_like(v2)
            probs, stats = [], []
            for half in range(2):
                keep = lo if half == 0 else jnp.logical_not(lo)
                qh = jnp.where(keep, q2, 0.0).astype(BF16)
                s = lax.dot_general(qh, k2, nt, preferred_element_type=F32)
                s = s + bias_ref[bias_sel, 2 * hp + half]
                m = jnp.max(s, axis=-1, keepdims=True)
                p = jnp.exp(s - m)
                probs.append(p.astype(BF16))
                stats.append((m, jnp.sum(p, axis=-1, keepdims=True)))
            v_stack = jnp.concatenate([jnp.where(lo, v2, zero), jnp.where(lo, zero, v2)], axis=0)
            o = jnp.dot(jnp.concatenate(probs, axis=1), v_stack, preferred_element_type=F32)
            (m0, l0), (m1, l1) = stats
            o_ref[pl.ds(r0, ATTN_BLOCK), cols] = (o * jnp.where(lo, 1.0 / l0, 1.0 / l1)).astype(BF16)
            lse_ref[pl.ds(r0, ATTN_BLOCK), cols] = jnp.where(lo, m0 + jnp.log(l0), m1 + jnp.log(l1))
        return carry

    lax.fori_loop(0, ATTN_Q_TILE // ATTN_BLOCK, block, 0)


def _attn_pattern(h, batch, seq, dilation, bias):
    sub_len = seq // dilation
    hv = h.reshape(batch, sub_len, dilation * IN_CH)
    blocks_per_tile = ATTN_Q_TILE // ATTN_BLOCK

    def cur(col):
        return pl.BlockSpec((None, ATTN_Q_TILE, COL_BLOCK),
                            lambda b, r, n: (b, n, r * H_BLOCKS + col))

    def prev(col):
        return pl.BlockSpec((None, ATTN_BLOCK, COL_BLOCK),
                            lambda b, r, n: (b, jnp.maximum(n * blocks_per_tile - 1, 0),
                                             r * H_BLOCKS + col))

    out_spec = pl.BlockSpec((None, ATTN_Q_TILE, ATTN_CH), lambda b, r, n: (b, n, r))
    o, lse = pl.pallas_call(
        _attn_body,
        grid=(batch, dilation, sub_len // ATTN_Q_TILE),
        in_specs=[cur(COL_Q), cur(COL_K), prev(COL_K), cur(COL_V), prev(COL_V),
                  _const_spec(bias.shape)],
        out_specs=[out_spec, out_spec],
        out_shape=[jax.ShapeDtypeStruct((batch, sub_len, dilation * ATTN_CH), BF16),
                   jax.ShapeDtypeStruct((batch, sub_len, dilation * ATTN_CH), F32)],
        scratch_shapes=[pltpu.VMEM((ATTN_BLOCK + ATTN_Q_TILE, ATTN_CH), BF16),
                        pltpu.VMEM((ATTN_BLOCK + ATTN_Q_TILE, ATTN_CH), BF16)],
        compiler_params=_params("parallel", "parallel", "arbitrary"),
        name=f"attn_d{dilation}",
    )(hv, hv, hv, hv, hv, bias)
    return o.reshape(batch * seq, ATTN_CH), lse.reshape(batch * seq, ATTN_CH)


def _t5_bucket(dist):
    max_exact = N_BUCKETS // 2
    d = np.maximum(dist, 1).astype(np.float64)
    large = max_exact + (np.log(d / max_exact) / math.log(MAX_DISTANCE / max_exact)
                         * (N_BUCKETS - max_exact)).astype(np.int32)
    large = np.minimum(large, N_BUCKETS - 1)
    return np.where(dist < max_exact, dist, large).astype(np.int32)


def _attn_bias(rel_table, window, dilation):
    qi = np.arange(ATTN_BLOCK)[:, None]
    kj = np.arange(2 * ATTN_BLOCK)[None, :]
    dist = qi + ATTN_BLOCK - kj
    valid = (dist >= 0) & (dist <= window // dilation)
    bucket = _t5_bucket(np.clip(dist, 0, None) * dilation)
    bias = jnp.transpose(rel_table[bucket].astype(F32), (2, 0, 1))
    main = jnp.where(jnp.asarray(valid)[None], bias, NEG_LOGIT)
    first = jnp.where(jnp.asarray(valid & (kj >= ATTN_BLOCK))[None], bias, NEG_LOGIT)
    return jnp.stack([first, main])


def _outproj_body(alpha, ac_ref, o1_ref, o2_ref, o3_ref, l1_ref, l2_ref, l3_ref, x_ref,
                  w_ref, b_ref, g_ref, beta_ref, y_ref):
    l1, l2, l3 = l1_ref[...], l2_ref[...], l3_ref[...]
    big = jnp.maximum(jnp.maximum(l1, l2), l3)
    e1, e2, e3 = jnp.exp(l1 - big), jnp.exp(l2 - big), jnp.exp(l3 - big)
    num = (e1 * o1_ref[...].astype(F32) + e2 * o2_ref[...].astype(F32)
           + e3 * o3_ref[...].astype(F32))
    attn = (num / (e1 + e2 + e3)).astype(BF16)
    half = CONV_CH + GMLP_CH
    mix = (jnp.dot(ac_ref[...], w_ref[0:half, :], preferred_element_type=F32)
           + jnp.dot(attn, w_ref[half:, :], preferred_element_type=F32) + b_ref[...])
    y_ref[...] = _layernorm(alpha * x_ref[...] + mix, g_ref[...], beta_ref[...])


def _outproj(alpha, ac, outs, lses, x, w, b, g, beta):
    n = x.shape[0]
    row = lambda width: pl.BlockSpec((ROW_TILE, width), lambda i: (i, 0))
    return pl.pallas_call(
        functools.partial(_outproj_body, alpha),
        grid=(n // ROW_TILE,),
        in_specs=[row(CONV_CH + GMLP_CH)] + [row(ATTN_CH)] * 6 + [row(D_MODEL)]
                 + [_const_spec(w.shape), _const_spec(b.shape), _const_spec(g.shape),
                    _const_spec(beta.shape)],
        out_specs=row(D_MODEL),
        out_shape=jax.ShapeDtypeStruct((n, D_MODEL), F32),
        compiler_params=_params("parallel"),
        name="outproj_ln",
    )(ac, *outs, *lses, x, w, b, g, beta)


def _ffn_body(alpha, tiles_per_seq, x_ref, wup_ref, bup_ref, cw_ref, cb_ref, wdn_ref, bdn_ref,
              g_ref, beta_ref, y_ref, h_buf, tail_ref):
    first = (pl.program_id(0) % tiles_per_seq) == 0
    x = x_ref[...]
    xb = x.astype(BF16)
    width = 2 * FF_CHUNK
    pad = 8
    acc = alpha * x + bdn_ref[...]
    for c in range(D_FF // FF_CHUNK):
        cols = slice(c * width, (c + 1) * width)
        h = jnp.dot(xb, wup_ref[:, cols], preferred_element_type=F32) + bup_ref[:, cols]
        h_buf[0:pad, :] = jnp.where(first, 0.0, tail_ref[:, cols])
        h_buf[pad:, :] = h
        tail_ref[:, cols] = h[ROW_TILE - pad:, :]
        y = cb_ref[:, cols]
        for k in range(FFN_CONV_WIDTH):
            shift = FFN_CONV_WIDTH - 1 - k
            y = y + cw_ref[k:k + 1, cols] * h_buf[pad - shift:pad - shift + ROW_TILE, :]
        gate = y[:, :FF_CHUNK]
        act = (gate * _sigmoid(gate) * y[:, FF_CHUNK:]).astype(BF16)
        acc = acc + jnp.dot(act, wdn_ref[c * FF_CHUNK:(c + 1) * FF_CHUNK, :],
                            preferred_element_type=F32)
    y_ref[...] = _layernorm(acc, g_ref[...], beta_ref[...])


def _ffn(alpha, x, seq, wup, bup, cw, cb, wdn, bdn, g, beta):
    n = x.shape[0]
    row = pl.BlockSpec((ROW_TILE, D_MODEL), lambda i: (i, 0))
    consts = [wup, bup, cw, cb, wdn, bdn, g, beta]
    return pl.pallas_call(
        functools.partial(_ffn_body, alpha, seq // ROW_TILE),
        grid=(n // ROW_TILE,),
        in_specs=[row] + [_const_spec(c.shape) for c in consts],
        out_specs=row,
        out_shape=jax.ShapeDtypeStruct((n, D_MODEL), F32),
        scratch_shapes=[pltpu.VMEM((8 + ROW_TILE, 2 * FF_CHUNK), F32),
                        pltpu.VMEM((8, 2 * D_FF), F32)],
        compiler_params=_params("arbitrary"),
        name="convffn_ln",
    )(x, *consts)


def _ffn_interleave(v):
    lead = v.shape[:-1]
    v = v.reshape(lead + (2, D_FF // FF_CHUNK, FF_CHUNK))
    return jnp.swapaxes(v, -3, -2).reshape(lead + (2 * D_FF,))


def kernel(x, w_in, b_in, conv_dw_w, conv_dw_b, conv_ln_g, conv_ln_b, rel_bias_table, gmlp_ln_g, gmlp_ln_b, gmlp_w_s, gmlp_b_s, w_out, b_out, ln1_g, ln1_b, ffn_w_up, ffn_b_up, ffn_conv_w, ffn_conv_b, ffn_w_down, ffn_b_down, ln2_g, ln2_b):
    batch, seq, _ = x.shape
    depth = w_in.shape[0]
    alpha = (2.0 * depth) ** 0.25
    n = batch * seq
    row2d = lambda v: v.reshape(1, -1)
    biases = [_attn_bias(rel_bias_table, w, d) for (w, d) in DILATED_PATTERNS]

    xf = x.reshape(n, D_MODEL)
    for l in range(depth):
        h = _inproj(xf, w_in[l].astype(BF16), row2d(b_in[l]))
        ws_cat = jnp.transpose(jnp.tril(gmlp_w_s[l]), (1, 0, 2)).reshape(CHUNK, GMLP_GROUPS * CHUNK)
        bs_tab = jnp.repeat(gmlp_b_s[l].T, GMLP_GROUP_DIM, axis=1)
        ac = _conv_gmlp(h, seq, conv_dw_w[l], row2d(conv_dw_b[l]), row2d(conv_ln_g[l]),
                        row2d(conv_ln_b[l]), row2d(gmlp_ln_g[l]), row2d(gmlp_ln_b[l]),
                        ws_cat.astype(BF16), bs_tab)
        outs, lses = [], []
        for (_, d), bias in zip(DILATED_PATTERNS, biases):
            o, lse = _attn_pattern(h, batch, seq, d, bias)
            outs.append(o)
            lses.append(lse)
        wo = w_out[l].astype(BF16)
        wo = jnp.concatenate([wo[:CONV_CH], wo[CONV_CH + ATTN_CH:], wo[CONV_CH:CONV_CH + ATTN_CH]])
        x1 = _outproj(alpha, ac, outs, lses, xf, wo,
                      row2d(b_out[l]), row2d(ln1_g[l]), row2d(ln1_b[l]))
        xf = _ffn(alpha, x1, seq, _ffn_interleave(ffn_w_up[l].astype(BF16)),
                  row2d(_ffn_interleave(ffn_b_up[l])), _ffn_interleave(ffn_conv_w[l]),
                  row2d(_ffn_interleave(ffn_conv_b[l])), ffn_w_down[l].astype(BF16),
                  row2d(ffn_b_down[l]), row2d(ln2_g[l]), row2d(ln2_b[l]))
    return xf.reshape(batch, seq, D_MODEL)
```

```python
import functools
import math

import jax
import jax.numpy as jnp
import numpy as np
from jax import lax
from jax.experimental import pallas as pl
from jax.experimental.pallas import tpu as pltpu

D_MODEL = 1024
HEAD_DIM = 64
CONV_CH = 256
CONV_WIDTH = 31
ATTN_HEADS = 8
ATTN_CH = ATTN_HEADS * HEAD_DIM
DILATED_PATTERNS = ((128, 1), (512, 4), (2048, 16))
ATTN_BLOCK = 128
N_BUCKETS = 32
MAX_DISTANCE = 2048
GMLP_CH = 256
GMLP_GROUPS = 4
GMLP_GROUP_DIM = GMLP_CH // GMLP_GROUPS
CHUNK = 128
MIX_CH = CONV_CH + ATTN_CH + GMLP_CH
IN_CH = 2 * CONV_CH + 3 * ATTN_CH + 2 * GMLP_CH
D_FF = 2816
FFN_CONV_WIDTH = 3
LN_EPS = 1e-5

LANES = 128
COL_BLOCK = 512
H_BLOCKS = IN_CH // COL_BLOCK
COL_A, COL_Q, COL_K, COL_V, COL_C = range(5)

ROW_TILE = 512
CONV_HALO = 32
CONV_ROW_CHUNK = 64
HEAD_PAIR = 2 * HEAD_DIM
N_PAIRS = ATTN_HEADS // 2
ATTN_TILE = ATTN_BLOCK * max(d for _, d in DILATED_PATTERNS)
MERGE_ROWS = 256
FF_CHUNK = 256
NEG_LOGIT = -1e30
VMEM_LIMIT_BYTES = 56 * 1024 * 1024

BF16 = jnp.bfloat16
F32 = jnp.float32


def _const_spec(shape):
    zeros = (0,) * len(shape)
    return pl.BlockSpec(shape, lambda *_: zeros, pipeline_mode=pl.Buffered(1))


def _params(*semantics):
    return pltpu.CompilerParams(dimension_semantics=semantics,
                                vmem_limit_bytes=VMEM_LIMIT_BYTES)


def _layernorm(x, g, b):
    mu = jnp.mean(x, axis=-1, keepdims=True)
    xc = x - mu
    var = jnp.mean(xc * xc, axis=-1, keepdims=True)
    return xc * lax.rsqrt(var + LN_EPS) * g + b


def _sigmoid(x):
    return 1.0 / (1.0 + jnp.exp(-x))


def _inproj_body(x_ref, w_ref, b_ref, h_ref):
    xb = x_ref[...].astype(BF16)
    for j in range(H_BLOCKS):
        cols = slice(j * COL_BLOCK, (j + 1) * COL_BLOCK)
        h_ref[:, cols] = (jnp.dot(xb, w_ref[:, cols], preferred_element_type=F32)
                          + b_ref[:, cols])


def _inproj(x, w, b):
    n = x.shape[0]
    return pl.pallas_call(
        _inproj_body,
        grid=(n // ROW_TILE,),
        in_specs=[pl.BlockSpec((ROW_TILE, D_MODEL), lambda i: (i, 0)),
                  _const_spec((D_MODEL, IN_CH)),
                  _const_spec((1, IN_CH))],
        out_specs=pl.BlockSpec((ROW_TILE, IN_CH), lambda i: (i, 0)),
        out_shape=jax.ShapeDtypeStruct((n, IN_CH), F32),
        compiler_params=_params("parallel"),
        name="inproj",
    )(x, w, b)


def _conv_gmlp_body(tiles_per_seq, a_ref, halo_ref, c_ref, dww_ref, dwb_ref, cg_ref, cb_ref,
                    gg_ref, gb_ref, ws_ref, bs_ref, o_ref, buf_ref):
    first = (pl.program_id(0) % tiles_per_seq) == 0

    halo = halo_ref[...]
    glu_halo = halo[:, :CONV_CH] * _sigmoid(halo[:, CONV_CH:])
    buf_ref[0:CONV_HALO, :] = jnp.where(first, 0.0, glu_halo)
    a = a_ref[...]
    buf_ref[CONV_HALO:, :] = a[:, :CONV_CH] * _sigmoid(a[:, CONV_CH:])

    lead = CONV_HALO - (CONV_WIDTH - 1)

    for r0 in range(0, ROW_TILE, CONV_ROW_CHUNK):
        acc = jnp.broadcast_to(dwb_ref[...], (CONV_ROW_CHUNK, CONV_CH))
        for k in range(CONV_WIDTH):
            start = r0 + lead + k
            acc = acc + dww_ref[k:k + 1, :] * buf_ref[start:start + CONV_ROW_CHUNK, :]
        y = _layernorm(acc, cg_ref[...], cb_ref[...])
        o_ref[r0:r0 + CONV_ROW_CHUNK, 0:CONV_CH] = (y * _sigmoid(y)).astype(BF16)

    lane = lax.broadcasted_iota(jnp.int32, (1, GMLP_CH), 1)
    for ch in range(ROW_TILE // CHUNK):
        rows = slice(ch * CHUNK, (ch + 1) * CHUNK)
        c = c_ref[rows, :]
        vn = _layernorm(c[:, GMLP_CH:], gg_ref[...], gb_ref[...])
        stacked = jnp.concatenate(
            [jnp.where((lane // GMLP_GROUP_DIM) == g, vn, 0.0) for g in range(GMLP_GROUPS)],
            axis=0).astype(BF16)
        mixed = jnp.dot(ws_ref[...], stacked, preferred_element_type=F32) + bs_ref[...]
        o_ref[rows, CONV_CH:] = (c[:, :GMLP_CH] * mixed).astype(BF16)


def _conv_gmlp(h, seq, dww, dwb, cg, cb, gg, gb, ws_cat, bs_tab):
    n = h.shape[0]
    tiles_per_seq = seq // ROW_TILE
    halo_per_tile = ROW_TILE // CONV_HALO
    return pl.pallas_call(
        functools.partial(_conv_gmlp_body, tiles_per_seq),
        grid=(n // ROW_TILE,),
        in_specs=[pl.BlockSpec((ROW_TILE, COL_BLOCK), lambda i: (i, COL_A)),
                  pl.BlockSpec((CONV_HALO, COL_BLOCK),
                               lambda i: (jnp.maximum(i * halo_per_tile - 1, 0), COL_A)),
                  pl.BlockSpec((ROW_TILE, COL_BLOCK), lambda i: (i, COL_C)),
                  _const_spec(dww.shape), _const_spec(dwb.shape),
                  _const_spec(cg.shape), _const_spec(cb.shape),
                  _const_spec(gg.shape), _const_spec(gb.shape),
                  _const_spec(ws_cat.shape), _const_spec(bs_tab.shape)],
        out_specs=pl.BlockSpec((ROW_TILE, CONV_CH + GMLP_CH), lambda i: (i, 0)),
        out_shape=jax.ShapeDtypeStruct((n, CONV_CH + GMLP_CH), BF16),
        scratch_shapes=[pltpu.VMEM((CONV_HALO + ROW_TILE, CONV_CH), F32)],
        compiler_params=_params("parallel"),
        name="conv_gmlp",
    )(h, h, h, dww, dwb, cg, cb, gg, gb, ws_cat, bs_tab)


def _attn_unit(q2, k_parts, v_parts, bias_lo, bias_hi):
    lo = lax.broadcasted_iota(jnp.int32, (1, HEAD_PAIR), 1) < HEAD_DIM
    nt = (((1,), (1,)), ((), ()))
    q2 = q2 * (HEAD_DIM ** -0.5)
    k2 = jnp.concatenate(k_parts, axis=0).astype(BF16)
    v2 = jnp.concatenate(v_parts, axis=0).astype(BF16)
    zero = jnp.zeros_like(v2)
    probs, stats = [], []
    for keep, bias in ((lo, bias_lo), (jnp.logical_not(lo), bias_hi)):
        qh = jnp.where(keep, q2, 0.0).astype(BF16)
        s = lax.dot_general(qh, k2, nt, preferred_element_type=F32) + bias
        m = jnp.max(s, axis=-1, keepdims=True)
        p = jnp.exp(s - m)
        probs.append(p.astype(BF16))
        stats.append((m, jnp.sum(p, axis=-1, keepdims=True)))
    v_stack = jnp.concatenate([jnp.where(lo, v2, zero), jnp.where(lo, zero, v2)], axis=0)
    o = jnp.dot(jnp.concatenate(probs, axis=1), v_stack, preferred_element_type=F32)
    (m0, l0), (m1, l1) = stats
    return (o * jnp.where(lo, 1.0 / l0, 1.0 / l1),
            jnp.where(lo, m0 + jnp.log(l0), m1 + jnp.log(l1)))


def _attn_body(q_ref, kc_ref, kp_ref, vc_ref, vp_ref, bias_ref, out_ref, o_scr, lse_scr):
    pair = pl.program_id(1)
    edge = jnp.where(pl.program_id(2) == 0, 0, 1)

    for p, (_, d) in enumerate(DILATED_PATTERNS):
        blocks = ATTN_TILE // (d * ATTN_BLOCK)
        span = d * ATTN_BLOCK

        def rows(start, n=ATTN_BLOCK, d=d):
            return pl.ds(start, n) if d == 1 else pl.ds(start, n, stride=d)

        def bias_pair(sel, p=p):
            return bias_ref[p, sel, 2 * pair], bias_ref[p, sel, 2 * pair + 1]

        def finish(start, result, p=p, rows=rows):
            o_scr[p, rows(start), :] = result[0]
            lse_scr[p, rows(start), :] = result[1]

        def first_block(r, rows=rows, span=span, bias_pair=bias_pair, finish=finish):
            tail = ATTN_TILE - span + r
            finish(r, _attn_unit(q_ref[rows(r), :],
                                 [kp_ref[rows(tail), :], kc_ref[rows(r), :]],
                                 [vp_ref[rows(tail), :], vc_ref[rows(r), :]],
                                 *bias_pair(edge)))

        def later_block(r, j, rows=rows, span=span, bias_pair=bias_pair, finish=finish):
            start = r + j * span
            keys = rows(start - span, 2 * ATTN_BLOCK)
            k2, v2 = kc_ref[keys, :], vc_ref[keys, :]
            finish(start, _attn_unit(q_ref[rows(start), :],
                                     [k2[:ATTN_BLOCK], k2[ATTN_BLOCK:]],
                                     [v2[:ATTN_BLOCK], v2[ATTN_BLOCK:]],
                                     *bias_pair(1)))

        def subsequence(r, carry, blocks=blocks, first_block=first_block,
                        later_block=later_block):
            first_block(r)
            if blocks > 1:
                lax.fori_loop(1, blocks, lambda j, c: (later_block(r, j), c)[1], 0)
            return carry

        if d == 1:
            subsequence(0, 0)
        else:
            lax.fori_loop(0, d, subsequence, 0)

    def merge(i, carry):
        r = pl.ds(pl.multiple_of(i * MERGE_ROWS, MERGE_ROWS), MERGE_ROWS)
        l0, l1, l2 = lse_scr[0, r, :], lse_scr[1, r, :], lse_scr[2, r, :]
        big = jnp.maximum(jnp.maximum(l0, l1), l2)
        e0, e1, e2 = jnp.exp(l0 - big), jnp.exp(l1 - big), jnp.exp(l2 - big)
        num = e0 * o_scr[0, r, :] + e1 * o_scr[1, r, :] + e2 * o_scr[2, r, :]
        out_ref[r, :] = (num / (e0 + e1 + e2)).astype(BF16)
        return carry

    lax.fori_loop(0, ATTN_TILE // MERGE_ROWS, merge, 0)


def _attention(h, batch, seq, bias):
    n = h.shape[0]
    tiles = seq // ATTN_TILE
    lane_blocks = COL_BLOCK // LANES

    def cur(col):
        return pl.BlockSpec((ATTN_TILE, LANES),
                            lambda b, hp, t: (b * tiles + t, col * lane_blocks + hp))

    def prev(col):
        return pl.BlockSpec((ATTN_TILE, LANES),
                            lambda b, hp, t: (b * tiles + jnp.maximum(t - 1, 0),
                                              col * lane_blocks + hp))

    return pl.pallas_call(
        _attn_body,
        grid=(batch, N_PAIRS, tiles),
        in_specs=[cur(COL_Q), cur(COL_K), prev(COL_K), cur(COL_V), prev(COL_V),
                  _const_spec(bias.shape)],
        out_specs=pl.BlockSpec((ATTN_TILE, LANES), lambda b, hp, t: (b * tiles + t, hp)),
        out_shape=jax.ShapeDtypeStruct((n, ATTN_CH), BF16),
        scratch_shapes=[pltpu.VMEM((len(DILATED_PATTERNS), ATTN_TILE, LANES), F32),
                        pltpu.VMEM((len(DILATED_PATTERNS), ATTN_TILE, LANES), F32)],
        compiler_params=_params("parallel", "parallel", "arbitrary"),
        name="dilated_attn",
    )(h, h, h, h, h, bias)


def _t5_bucket(dist):
    max_exact = N_BUCKETS // 2
    d = np.maximum(dist, 1).astype(np.float64)
    large = max_exact + (np.log(d / max_exact) / math.log(MAX_DISTANCE / max_exact)
                         * (N_BUCKETS - max_exact)).astype(np.int32)
    large = np.minimum(large, N_BUCKETS - 1)
    return np.where(dist < max_exact, dist, large).astype(np.int32)


def _attn_bias(rel_table, window, dilation):
    n_win = window // dilation
    assert n_win <= ATTN_BLOCK
    heads = rel_table.shape[1]
    onehot = np.eye(N_BUCKETS, dtype=np.float32)[_t5_bucket(np.arange(n_win + 1) * dilation)]
    by_dist = jnp.sum(onehot[:, :, None] * rel_table.astype(F32)[None], axis=1)
    period = 3 * ATTN_BLOCK + 1
    u = jnp.concatenate([jnp.full((heads, ATTN_BLOCK - n_win), NEG_LOGIT, F32), by_dist[::-1].T,
                         jnp.full((heads, period - ATTN_BLOCK - 1), NEG_LOGIT, F32)], axis=1)
    toeplitz = jnp.tile(u, (1, ATTN_BLOCK))[:, :ATTN_BLOCK * (period - 1)]
    main = toeplitz.reshape(heads, ATTN_BLOCK, period - 1)[:, :, :2 * ATTN_BLOCK]
    kj = np.arange(2 * ATTN_BLOCK)[None, None, :]
    first = jnp.where(jnp.asarray(kj >= ATTN_BLOCK), main, NEG_LOGIT)
    return jnp.stack([first, main])


def _outproj_body(alpha, ac_ref, attn_ref, x_ref, w_ref, b_ref, g_ref, beta_ref, y_ref):
    half = CONV_CH + GMLP_CH
    mix = (jnp.dot(ac_ref[...], w_ref[0:half, :], preferred_element_type=F32)
           + jnp.dot(attn_ref[...], w_ref[half:, :], preferred_element_type=F32) + b_ref[...])
    y_ref[...] = _layernorm(alpha * x_ref[...] + mix, g_ref[...], beta_ref[...])


def _outproj(alpha, ac, attn, x, w, b, g, beta):
    n = x.shape[0]
    row = lambda width: pl.BlockSpec((ROW_TILE, width), lambda i: (i, 0))
    return pl.pallas_call(
        functools.partial(_outproj_body, alpha),
        grid=(n // ROW_TILE,),
        in_specs=[row(CONV_CH + GMLP_CH), row(ATTN_CH), row(D_MODEL),
                  _const_spec(w.shape), _const_spec(b.shape), _const_spec(g.shape),
                  _const_spec(beta.shape)],
        out_specs=row(D_MODEL),
        out_shape=jax.ShapeDtypeStruct((n, D_MODEL), F32),
        compiler_params=_params("parallel"),
        name="outproj_ln",
    )(ac, attn, x, w, b, g, beta)


def _ffn_body(alpha, tiles_per_seq, x_ref, wup_ref, bup_ref, cw_ref, cb_ref, wdn_ref, bdn_ref,
              g_ref, beta_ref, y_ref, h_buf, tail_ref):
    first = (pl.program_id(0) % tiles_per_seq) == 0
    x = x_ref[...]
    xb = x.astype(BF16)
    width = 2 * FF_CHUNK
    pad = 8
    acc = alpha * x + bdn_ref[...]
    for c in range(D_FF // FF_CHUNK):
        cols = slice(c * width, (c + 1) * width)
        h = jnp.dot(xb, wup_ref[:, cols], preferred_element_type=F32) + bup_ref[:, cols]
        h_buf[0:pad, :] = jnp.where(first, 0.0, tail_ref[:, cols])
        h_buf[pad:, :] = h
        tail_ref[:, cols] = h[ROW_TILE - pad:, :]
        y = cb_ref[:, cols]
        for k in range(FFN_CONV_WIDTH):
            shift = FFN_CONV_WIDTH - 1 - k
            y = y + cw_ref[k:k + 1, cols] * h_buf[pad - shift:pad - shift + ROW_TILE, :]
        gate = y[:, :FF_CHUNK]
        act = (gate * _sigmoid(gate) * y[:, FF_CHUNK:]).astype(BF16)
        acc = acc + jnp.dot(act, wdn_ref[c * FF_CHUNK:(c + 1) * FF_CHUNK, :],
                            preferred_element_type=F32)
    y_ref[...] = _layernorm(acc, g_ref[...], beta_ref[...])


def _ffn(alpha, x, seq, wup, bup, cw, cb, wdn, bdn, g, beta):
    n = x.shape[0]
    row = pl.BlockSpec((ROW_TILE, D_MODEL), lambda i: (i, 0))
    consts = [wup, bup, cw, cb, wdn, bdn, g, beta]
    return pl.pallas_call(
        functools.partial(_ffn_body, alpha, seq // ROW_TILE),
        grid=(n // ROW_TILE,),
        in_specs=[row] + [_const_spec(c.shape) for c in consts],
        out_specs=row,
        out_shape=jax.ShapeDtypeStruct((n, D_MODEL), F32),
        scratch_shapes=[pltpu.VMEM((8 + ROW_TILE, 2 * FF_CHUNK), F32),
                        pltpu.VMEM((8, 2 * D_FF), F32)],
        compiler_params=_params("arbitrary"),
        name="convffn_ln",
    )(x, *consts)


def _ffn_interleave(v):
    lead = v.shape[:-1]
    v = v.reshape(lead + (2, D_FF // FF_CHUNK, FF_CHUNK))
    return jnp.swapaxes(v, -3, -2).reshape(lead + (2 * D_FF,))


def kernel(x, w_in, b_in, conv_dw_w, conv_dw_b, conv_ln_g, conv_ln_b, rel_bias_table, gmlp_ln_g, gmlp_ln_b, gmlp_w_s, gmlp_b_s, w_out, b_out, ln1_g, ln1_b, ffn_w_up, ffn_b_up, ffn_conv_w, ffn_conv_b, ffn_w_down, ffn_b_down, ln2_g, ln2_b):
    batch, seq, _ = x.shape
    depth = w_in.shape[0]
    alpha = (2.0 * depth) ** 0.25
    n = batch * seq
    row2d = lambda v: v.reshape(1, -1)
    bias = jnp.stack([_attn_bias(rel_bias_table, w, d) for (w, d) in DILATED_PATTERNS])

    xf = x.reshape(n, D_MODEL)
    for l in range(depth):
        h = _inproj(xf, w_in[l].astype(BF16), row2d(b_in[l]))
        ws_cat = jnp.transpose(jnp.tril(gmlp_w_s[l]), (1, 0, 2)).reshape(CHUNK, GMLP_GROUPS * CHUNK)
        bs_tab = jnp.repeat(gmlp_b_s[l].T, GMLP_GROUP_DIM, axis=1)
        ac = _conv_gmlp(h, seq, conv_dw_w[l], row2d(conv_dw_b[l]), row2d(conv_ln_g[l]),
                        row2d(conv_ln_b[l]), row2d(gmlp_ln_g[l]), row2d(gmlp_ln_b[l]),
                        ws_cat.astype(BF16), bs_tab)
        attn = _attention(h, batch, seq, bias)
        wo = w_out[l].astype(BF16)
        wo = jnp.concatenate([wo[:CONV_CH], wo[CONV_CH + ATTN_CH:], wo[CONV_CH:CONV_CH + ATTN_CH]])
        x1 = _outproj(alpha, ac, attn, xf, wo, row2d(b_out[l]), row2d(ln1_g[l]), row2d(ln1_b[l]))
        xf = _ffn(alpha, x1, seq, _ffn_interleave(ffn_w_up[l].astype(BF16)),
                  row2d(_ffn_interleave(ffn_b_up[l])), _ffn_interleave(ffn_conv_w[l]),
                  row2d(_ffn_interleave(ffn_conv_b[l])), ffn_w_down[l].astype(BF16),
                  row2d(ffn_b_down[l]), row2d(ln2_g[l]), row2d(ln2_b[l]))
    return xf.reshape(batch, seq, D_MODEL)
```

```python
import functools
import math

import jax
import jax.numpy as jnp
import numpy as np
from jax import lax
from jax.experimental import pallas as pl
from jax.experimental.pallas import tpu as pltpu

D_MODEL = 1024
HEAD_DIM = 64
CONV_CH = 256
CONV_WIDTH = 31
ATTN_HEADS = 8
ATTN_CH = ATTN_HEADS * HEAD_DIM
DILATED_PATTERNS = ((128, 1), (512, 4), (2048, 16))
ATTN_BLOCK = 128
N_BUCKETS = 32
MAX_DISTANCE = 2048
GMLP_CH = 256
GMLP_GROUPS = 4
GMLP_GROUP_DIM = GMLP_CH // GMLP_GROUPS
CHUNK = 128
MIX_CH = CONV_CH + ATTN_CH + GMLP_CH
IN_CH = 2 * CONV_CH + 3 * ATTN_CH + 2 * GMLP_CH
D_FF = 2816
FFN_CONV_WIDTH = 3
LN_EPS = 1e-5

LANES = 128
COL_BLOCK = 512
H_BLOCKS = IN_CH // COL_BLOCK
COL_A, COL_Q, COL_K, COL_V, COL_C = range(5)

ROW_TILE = 512
CONV_HALO = 32
CONV_ROW_CHUNK = 64
HEAD_PAIR = 2 * HEAD_DIM
N_PAIRS = ATTN_HEADS // 2
ATTN_TILE = ATTN_BLOCK * max(d for _, d in DILATED_PATTERNS)
ATTN_GROUP = 4
MERGE_ROWS = 256
FF_CHUNK = 256
LOG2_E = math.log2(math.e)
NEG_LOGIT = -1e30
VMEM_LIMIT_BYTES = 56 * 1024 * 1024

BF16 = jnp.bfloat16
F32 = jnp.float32


def _const_spec(shape):
    zeros = (0,) * len(shape)
    return pl.BlockSpec(shape, lambda *_: zeros, pipeline_mode=pl.Buffered(1))


def _params(*semantics):
    return pltpu.CompilerParams(dimension_semantics=semantics,
                                vmem_limit_bytes=VMEM_LIMIT_BYTES)


def _layernorm(x, g, b):
    mu = jnp.mean(x, axis=-1, keepdims=True)
    xc = x - mu
    var = jnp.mean(xc * xc, axis=-1, keepdims=True)
    return xc * lax.rsqrt(var + LN_EPS) * g + b


def _sigmoid(x):
    return 1.0 / (1.0 + jnp.exp(-x))


def _inproj_body(x_ref, w_ref, b_ref, h_ref):
    xb = x_ref[...].astype(BF16)
    for j in range(H_BLOCKS):
        cols = slice(j * COL_BLOCK, (j + 1) * COL_BLOCK)
        h_ref[:, cols] = (jnp.dot(xb, w_ref[:, cols], preferred_element_type=F32)
                          + b_ref[:, cols])


def _inproj(x, w, b):
    n = x.shape[0]
    return pl.pallas_call(
        _inproj_body,
        grid=(n // ROW_TILE,),
        in_specs=[pl.BlockSpec((ROW_TILE, D_MODEL), lambda i: (i, 0)),
                  _const_spec((D_MODEL, IN_CH)),
                  _const_spec((1, IN_CH))],
        out_specs=pl.BlockSpec((ROW_TILE, IN_CH), lambda i: (i, 0)),
        out_shape=jax.ShapeDtypeStruct((n, IN_CH), F32),
        compiler_params=_params("parallel"),
        name="inproj",
    )(x, w, b)


def _conv_gmlp_body(tiles_per_seq, a_ref, halo_ref, c_ref, dww_ref, dwb_ref, cg_ref, cb_ref,
                    gg_ref, gb_ref, ws_ref, bs_ref, o_ref, buf_ref):
    first = (pl.program_id(0) % tiles_per_seq) == 0

    halo = halo_ref[...]
    glu_halo = halo[:, :CONV_CH] * _sigmoid(halo[:, CONV_CH:])
    buf_ref[0:CONV_HALO, :] = jnp.where(first, 0.0, glu_halo)
    a = a_ref[...]
    buf_ref[CONV_HALO:, :] = a[:, :CONV_CH] * _sigmoid(a[:, CONV_CH:])

    lead = CONV_HALO - (CONV_WIDTH - 1)

    for r0 in range(0, ROW_TILE, CONV_ROW_CHUNK):
        acc = jnp.broadcast_to(dwb_ref[...], (CONV_ROW_CHUNK, CONV_CH))
        for k in range(CONV_WIDTH):
            start = r0 + lead + k
            acc = acc + dww_ref[k:k + 1, :] * buf_ref[start:start + CONV_ROW_CHUNK, :]
        y = _layernorm(acc, cg_ref[...], cb_ref[...])
        o_ref[r0:r0 + CONV_ROW_CHUNK, 0:CONV_CH] = (y * _sigmoid(y)).astype(BF16)

    lane = lax.broadcasted_iota(jnp.int32, (1, GMLP_CH), 1)
    for ch in range(ROW_TILE // CHUNK):
        rows = slice(ch * CHUNK, (ch + 1) * CHUNK)
        c = c_ref[rows, :]
        vn = _layernorm(c[:, GMLP_CH:], gg_ref[...], gb_ref[...])
        stacked = jnp.concatenate(
            [jnp.where((lane // GMLP_GROUP_DIM) == g, vn, 0.0) for g in range(GMLP_GROUPS)],
            axis=0).astype(BF16)
        mixed = jnp.dot(ws_ref[...], stacked, preferred_element_type=F32) + bs_ref[...]
        o_ref[rows, CONV_CH:] = (c[:, :GMLP_CH] * mixed).astype(BF16)


def _conv_gmlp(h, seq, dww, dwb, cg, cb, gg, gb, ws_cat, bs_tab):
    n = h.shape[0]
    tiles_per_seq = seq // ROW_TILE
    halo_per_tile = ROW_TILE // CONV_HALO
    return pl.pallas_call(
        functools.partial(_conv_gmlp_body, tiles_per_seq),
        grid=(n // ROW_TILE,),
        in_specs=[pl.BlockSpec((ROW_TILE, COL_BLOCK), lambda i: (i, COL_A)),
                  pl.BlockSpec((CONV_HALO, COL_BLOCK),
                               lambda i: (jnp.maximum(i * halo_per_tile - 1, 0), COL_A)),
                  pl.BlockSpec((ROW_TILE, COL_BLOCK), lambda i: (i, COL_C)),
                  _const_spec(dww.shape), _const_spec(dwb.shape),
                  _const_spec(cg.shape), _const_spec(cb.shape),
                  _const_spec(gg.shape), _const_spec(gb.shape),
                  _const_spec(ws_cat.shape), _const_spec(bs_tab.shape)],
        out_specs=pl.BlockSpec((ROW_TILE, CONV_CH + GMLP_CH), lambda i: (i, 0)),
        out_shape=jax.ShapeDtypeStruct((n, CONV_CH + GMLP_CH), BF16),
        scratch_shapes=[pltpu.VMEM((CONV_HALO + ROW_TILE, CONV_CH), F32)],
        compiler_params=_params("parallel"),
        name="conv_gmlp",
    )(h, h, h, dww, dwb, cg, cb, gg, gb, ws_cat, bs_tab)


def _attn_scores(q2, k2, bias_lo, bias_hi):
    lo = lax.broadcasted_iota(jnp.int32, (1, HEAD_PAIR), 1) < HEAD_DIM
    nt = (((1,), (1,)), ((), ()))
    q2 = q2 * (HEAD_DIM ** -0.5 * LOG2_E)
    kb = k2.astype(BF16)
    ps, ms = [], []
    for keep, bias in ((lo, bias_lo), (jnp.logical_not(lo), bias_hi)):
        qh = jnp.where(keep, q2, 0.0).astype(BF16)
        s = lax.dot_general(qh, kb, nt, preferred_element_type=F32) + bias
        m = jnp.max(s, axis=-1, keepdims=True)
        ps.append(jnp.exp2(s - m).astype(BF16))
        ms.append(m)
    return jnp.concatenate(ps, axis=1), jnp.where(lo, ms[0], ms[1])


def _attn_values(p, v2):
    lo = lax.broadcasted_iota(jnp.int32, (1, HEAD_PAIR), 1) < HEAD_DIM
    shape = (2 * ATTN_BLOCK, HEAD_PAIR)
    ind_lo = jnp.broadcast_to(jnp.where(lo, 1.0, 0.0), shape).astype(BF16)
    ind_hi = jnp.broadcast_to(jnp.where(lo, 0.0, 1.0), shape).astype(BF16)
    vb = v2.astype(BF16)
    zero = jnp.zeros_like(vb)
    v_stack = jnp.concatenate(
        [jnp.concatenate([jnp.where(lo, vb, zero), ind_lo], axis=1),
         jnp.concatenate([jnp.where(lo, zero, vb), ind_hi], axis=1)], axis=0)
    ol = jnp.dot(p, v_stack, preferred_element_type=F32)
    return ol[:, :HEAD_PAIR], ol[:, HEAD_PAIR:]


def _attn_body(q_ref, kc_ref, kp_ref, vc_ref, vp_ref, bias_ref, out_ref,
               k_all, v_all, p_scr, o_scr, m_scr, l_scr):
    pair = pl.program_id(1)
    edge = jnp.where(pl.program_id(2) == 0, 0, 1)
    k_all[0:ATTN_TILE, :] = kp_ref[...]
    k_all[ATTN_TILE:, :] = kc_ref[...]
    v_all[0:ATTN_TILE, :] = vp_ref[...]
    v_all[ATTN_TILE:, :] = vc_ref[...]

    def rows(start, n, d):
        return pl.ds(start, n) if d == 1 else pl.ds(start, n, stride=d)

    units = []
    for p, (_, d) in enumerate(DILATED_PATTERNS):
        span = d * ATTN_BLOCK
        for r in range(d):
            for j in range(ATTN_TILE // span):
                units.append((p, d, r + j * span, span, j == 0))
    groups = [units[g:g + ATTN_GROUP] for g in range(0, len(units), ATTN_GROUP)]

    def scores(group, slot):
        for t, (p, d, start, span, seq_edge) in enumerate(group):
            sel = edge if seq_edge else 1
            keys = rows(ATTN_TILE + start - span, 2 * ATTN_BLOCK, d)
            probs, m = _attn_scores(q_ref[rows(start, ATTN_BLOCK, d), :], k_all[keys, :],
                                    bias_ref[p, sel, 2 * pair], bias_ref[p, sel, 2 * pair + 1])
            p_scr[slot, t] = probs
            m_scr[p, rows(start, ATTN_BLOCK, d), :] = m

    def values(group, slot):
        for t, (p, d, start, span, _) in enumerate(group):
            keys = rows(ATTN_TILE + start - span, 2 * ATTN_BLOCK, d)
            o, l = _attn_values(p_scr[slot, t], v_all[keys, :])
            o_scr[p, rows(start, ATTN_BLOCK, d), :] = o
            l_scr[p, rows(start, ATTN_BLOCK, d), :] = l

    scores(groups[0], 0)
    for g in range(1, len(groups)):
        scores(groups[g], g % 2)
        values(groups[g - 1], (g - 1) % 2)
    values(groups[-1], (len(groups) - 1) % 2)

    def merge(i, carry):
        r = pl.ds(pl.multiple_of(i * MERGE_ROWS, MERGE_ROWS), MERGE_ROWS)
        ms = [m_scr[p, r, :] for p in range(len(DILATED_PATTERNS))]
        big = functools.reduce(jnp.maximum, ms)
        ws = [jnp.exp2(m - big) for m in ms]
        num = sum(w * o_scr[p, r, :] for p, w in enumerate(ws))
        den = sum(w * l_scr[p, r, :] for p, w in enumerate(ws))
        out_ref[r, :] = (num / den).astype(BF16)
        return carry

    lax.fori_loop(0, ATTN_TILE // MERGE_ROWS, merge, 0)


def _attention(h, batch, seq, bias):
    n = h.shape[0]
    tiles = seq // ATTN_TILE
    lane_blocks = COL_BLOCK // LANES

    def cur(col):
        return pl.BlockSpec((ATTN_TILE, LANES),
                            lambda b, hp, t: (b * tiles + t, col * lane_blocks + hp))

    def prev(col):
        return pl.BlockSpec((ATTN_TILE, LANES),
                            lambda b, hp, t: (b * tiles + jnp.maximum(t - 1, 0),
                                              col * lane_blocks + hp))

    return pl.pallas_call(
        _attn_body,
        grid=(batch, N_PAIRS, tiles),
        in_specs=[cur(COL_Q), cur(COL_K), prev(COL_K), cur(COL_V), prev(COL_V),
                  _const_spec(bias.shape)],
        out_specs=pl.BlockSpec((ATTN_TILE, LANES), lambda b, hp, t: (b * tiles + t, hp)),
        out_shape=jax.ShapeDtypeStruct((n, ATTN_CH), BF16),
        scratch_shapes=[pltpu.VMEM((2 * ATTN_TILE, LANES), F32),
                        pltpu.VMEM((2 * ATTN_TILE, LANES), F32),
                        pltpu.VMEM((2, ATTN_GROUP, ATTN_BLOCK, 4 * ATTN_BLOCK), BF16),
                        pltpu.VMEM((len(DILATED_PATTERNS), ATTN_TILE, LANES), F32),
                        pltpu.VMEM((len(DILATED_PATTERNS), ATTN_TILE, LANES), F32),
                        pltpu.VMEM((len(DILATED_PATTERNS), ATTN_TILE, LANES), F32)],
        compiler_params=_params("parallel", "parallel", "arbitrary"),
        name="dilated_attn",
    )(h, h, h, h, h, bias)


def _t5_bucket(dist):
    max_exact = N_BUCKETS // 2
    d = np.maximum(dist, 1).astype(np.float64)
    large = max_exact + (np.log(d / max_exact) / math.log(MAX_DISTANCE / max_exact)
                         * (N_BUCKETS - max_exact)).astype(np.int32)
    large = np.minimum(large, N_BUCKETS - 1)
    return np.where(dist < max_exact, dist, large).astype(np.int32)


def _attn_bias(rel_table, window, dilation):
    n_win = window // dilation
    assert n_win <= ATTN_BLOCK
    heads = rel_table.shape[1]
    onehot = np.eye(N_BUCKETS, dtype=np.float32)[_t5_bucket(np.arange(n_win + 1) * dilation)]
    by_dist = jnp.sum(onehot[:, :, None] * rel_table.astype(F32)[None], axis=1)
    period = 3 * ATTN_BLOCK + 1
    u = jnp.concatenate([jnp.full((heads, ATTN_BLOCK - n_win), NEG_LOGIT, F32), by_dist[::-1].T,
                         jnp.full((heads, period - ATTN_BLOCK - 1), NEG_LOGIT, F32)], axis=1)
    toeplitz = jnp.tile(u, (1, ATTN_BLOCK))[:, :ATTN_BLOCK * (period - 1)]
    main = toeplitz.reshape(heads, ATTN_BLOCK, period - 1)[:, :, :2 * ATTN_BLOCK]
    kj = np.arange(2 * ATTN_BLOCK)[None, None, :]
    first = jnp.where(jnp.asarray(kj >= ATTN_BLOCK), main, NEG_LOGIT)
    return jnp.stack([first, main]) * LOG2_E


def _outproj_body(alpha, ac_ref, attn_ref, x_ref, w_ref, b_ref, g_ref, beta_ref, y_ref):
    half = CONV_CH + GMLP_CH
    mix = (jnp.dot(ac_ref[...], w_ref[0:half, :], preferred_element_type=F32)
           + jnp.dot(attn_ref[...], w_ref[half:, :], preferred_element_type=F32) + b_ref[...])
    y_ref[...] = _layernorm(alpha * x_ref[...] + mix, g_ref[...], beta_ref[...])


def _outproj(alpha, ac, attn, x, w, b, g, beta):
    n = x.shape[0]
    row = lambda width: pl.BlockSpec((ROW_TILE, width), lambda i: (i, 0))
    return pl.pallas_call(
        functools.partial(_outproj_body, alpha),
        grid=(n // ROW_TILE,),
        in_specs=[row(CONV_CH + GMLP_CH), row(ATTN_CH), row(D_MODEL),
                  _const_spec(w.shape), _const_spec(b.shape), _const_spec(g.shape),
                  _const_spec(beta.shape)],
        out_specs=row(D_MODEL),
        out_shape=jax.ShapeDtypeStruct((n, D_MODEL), F32),
        compiler_params=_params("parallel"),
        name="outproj_ln",
    )(ac, attn, x, w, b, g, beta)


def _ffn_body(alpha, tiles_per_seq, x_ref, wup_ref, bup_ref, cw_ref, cb_ref, wdn_ref, bdn_ref,
              g_ref, beta_ref, y_ref, h_buf, tail_ref):
    first = (pl.program_id(0) % tiles_per_seq) == 0
    x = x_ref[...]
    xb = x.astype(BF16)
    width = 2 * FF_CHUNK
    pad = 8
    acc = alpha * x + bdn_ref[...]
    for c in range(D_FF // FF_CHUNK):
        cols = slice(c * width, (c + 1) * width)
        h = jnp.dot(xb, wup_ref[:, cols], preferred_element_type=F32) + bup_ref[:, cols]
        h_buf[0:pad, :] = jnp.where(first, 0.0, tail_ref[:, cols])
        h_buf[pad:, :] = h
        tail_ref[:, cols] = h[ROW_TILE - pad:, :]
        y = cb_ref[:, cols]
        for k in range(FFN_CONV_WIDTH):
            shift = FFN_CONV_WIDTH - 1 - k
            y = y + cw_ref[k:k + 1, cols] * h_buf[pad - shift:pad - shift + ROW_TILE, :]
        gate = y[:, :FF_CHUNK]
        act = (gate * _sigmoid(gate) * y[:, FF_CHUNK:]).astype(BF16)
        acc = acc + jnp.dot(act, wdn_ref[c * FF_CHUNK:(c + 1) * FF_CHUNK, :],
                            preferred_element_type=F32)
    y_ref[...] = _layernorm(acc, g_ref[...], beta_ref[...])


def _ffn(alpha, x, seq, wup, bup, cw, cb, wdn, bdn, g, beta):
    n = x.shape[0]
    row = pl.BlockSpec((ROW_TILE, D_MODEL), lambda i: (i, 0))
    consts = [wup, bup, cw, cb, wdn, bdn, g, beta]
    return pl.pallas_call(
        functools.partial(_ffn_body, alpha, seq // ROW_TILE),
        grid=(n // ROW_TILE,),
        in_specs=[row] + [_const_spec(c.shape) for c in consts],
        out_specs=row,
        out_shape=jax.ShapeDtypeStruct((n, D_MODEL), F32),
        scratch_shapes=[pltpu.VMEM((8 + ROW_TILE, 2 * FF_CHUNK), F32),
                        pltpu.VMEM((8, 2 * D_FF), F32)],
        compiler_params=_params("arbitrary"),
        name="convffn_ln",
    )(x, *consts)


def _ffn_interleave(v):
    lead = v.shape[:-1]
    v = v.reshape(lead + (2, D_FF // FF_CHUNK, FF_CHUNK))
    return jnp.swapaxes(v, -3, -2).reshape(lead + (2 * D_FF,))


def kernel(x, w_in, b_in, conv_dw_w, conv_dw_b, conv_ln_g, conv_ln_b, rel_bias_table, gmlp_ln_g, gmlp_ln_b, gmlp_w_s, gmlp_b_s, w_out, b_out, ln1_g, ln1_b, ffn_w_up, ffn_b_up, ffn_conv_w, ffn_conv_b, ffn_w_down, ffn_b_down, ln2_g, ln2_b):
    batch, seq, _ = x.shape
    depth = w_in.shape[0]
    alpha = (2.0 * depth) ** 0.25
    n = batch * seq
    row2d = lambda v: v.reshape(1, -1)
    bias = jnp.stack([_attn_bias(rel_bias_table, w, d) for (w, d) in DILATED_PATTERNS])

    xf = x.reshape(n, D_MODEL)
    for l in range(depth):
        h = _inproj(xf, w_in[l].astype(BF16), row2d(b_in[l]))
        ws_cat = jnp.transpose(jnp.tril(gmlp_w_s[l]), (1, 0, 2)).reshape(CHUNK, GMLP_GROUPS * CHUNK)
        bs_tab = jnp.repeat(gmlp_b_s[l].T, GMLP_GROUP_DIM, axis=1)
        ac = _conv_gmlp(h, seq, conv_dw_w[l], row2d(conv_dw_b[l]), row2d(conv_ln_g[l]),
                        row2d(conv_ln_b[l]), row2d(gmlp_ln_g[l]), row2d(gmlp_ln_b[l]),
                        ws_cat.astype(BF16), bs_tab)
        attn = _attention(h, batch, seq, bias)
        wo = w_out[l].astype(BF16)
        wo = jnp.concatenate([wo[:CONV_CH], wo[CONV_CH + ATTN_CH:], wo[CONV_CH:CONV_CH + ATTN_CH]])
        x1 = _outproj(alpha, ac, attn, xf, wo, row2d(b_out[l]), row2d(ln1_g[l]), row2d(ln1_b[l]))
        xf = _ffn(alpha, x1, seq, _ffn_interleave(ffn_w_up[l].astype(BF16)),
                  row2d(_ffn_interleave(ffn_b_up[l])), _ffn_interleave(ffn_conv_w[l]),
                  row2d(_ffn_interleave(ffn_conv_b[l])), ffn_w_down[l].astype(BF16),
                  row2d(ffn_b_down[l]), row2d(ln2_g[l]), row2d(ln2_b[l]))
    return xf.reshape(batch, seq, D_MODEL)
```

```python
import functools
import math

import jax
import jax.numpy as jnp
import numpy as np
from jax import lax
from jax.experimental import pallas as pl
from jax.experimental.pallas import tpu as pltpu

D_MODEL = 1024
HEAD_DIM = 64
CONV_CH = 256
CONV_WIDTH = 31
ATTN_HEADS = 8
ATTN_CH = ATTN_HEADS * HEAD_DIM
DILATED_PATTERNS = ((128, 1), (512, 4), (2048, 16))
ATTN_BLOCK = 128
N_BUCKETS = 32
MAX_DISTANCE = 2048
GMLP_CH = 256
GMLP_GROUPS = 4
GMLP_GROUP_DIM = GMLP_CH // GMLP_GROUPS
CHUNK = 128
MIX_CH = CONV_CH + ATTN_CH + GMLP_CH
IN_CH = 2 * CONV_CH + 3 * ATTN_CH + 2 * GMLP_CH
D_FF = 2816
FFN_CONV_WIDTH = 3
LN_EPS = 1e-5

LANES = 128
COL_BLOCK = 512
H_BLOCKS = IN_CH // COL_BLOCK
COL_A, COL_Q, COL_K, COL_V, COL_C = range(5)

ROW_TILE = 512
CONV_HALO = 32
CONV_ROW_CHUNK = 64
HEAD_PAIR = 2 * HEAD_DIM
N_PAIRS = ATTN_HEADS // 2
ATTN_TILE = ATTN_BLOCK * max(d for _, d in DILATED_PATTERNS)
ATTN_GROUP = 4
MERGE_ROWS = 256
FF_CHUNK = 256
FFN_LOOKAHEAD = 1
FFN_SLOTS = FFN_LOOKAHEAD + 1
FFN_ROWS = 64
FFN_HALO = 8
LOG2_E = math.log2(math.e)
NEG_LOGIT = -1e30
VMEM_LIMIT_BYTES = 56 * 1024 * 1024

BF16 = jnp.bfloat16
F32 = jnp.float32


def _const_spec(shape):
    zeros = (0,) * len(shape)
    return pl.BlockSpec(shape, lambda *_: zeros, pipeline_mode=pl.Buffered(1))


def _params(*semantics):
    return pltpu.CompilerParams(dimension_semantics=semantics,
                                vmem_limit_bytes=VMEM_LIMIT_BYTES)


def _layernorm(x, g, b):
    mu = jnp.mean(x, axis=-1, keepdims=True)
    xc = x - mu
    var = jnp.mean(xc * xc, axis=-1, keepdims=True)
    return xc * lax.rsqrt(var + LN_EPS) * g + b


def _sigmoid(x):
    return 1.0 / (1.0 + jnp.exp(-x))


def _inproj_body(x_ref, w_ref, b_ref, h_ref):
    xb = x_ref[...].astype(BF16)
    for j in range(H_BLOCKS):
        cols = slice(j * COL_BLOCK, (j + 1) * COL_BLOCK)
        h_ref[:, cols] = (jnp.dot(xb, w_ref[:, cols], preferred_element_type=F32)
                          + b_ref[:, cols])


def _inproj(x, w, b):
    n = x.shape[0]
    return pl.pallas_call(
        _inproj_body,
        grid=(n // ROW_TILE,),
        in_specs=[pl.BlockSpec((ROW_TILE, D_MODEL), lambda i: (i, 0)),
                  _const_spec((D_MODEL, IN_CH)),
                  _const_spec((1, IN_CH))],
        out_specs=pl.BlockSpec((ROW_TILE, IN_CH), lambda i: (i, 0)),
        out_shape=jax.ShapeDtypeStruct((n, IN_CH), F32),
        compiler_params=_params("parallel"),
        name="inproj",
    )(x, w, b)


def _conv_gmlp_body(tiles_per_seq, a_ref, halo_ref, c_ref, dww_ref, dwb_ref, cg_ref, cb_ref,
                    gg_ref, gb_ref, ws_ref, bs_ref, o_ref, buf_ref):
    first = (pl.program_id(0) % tiles_per_seq) == 0

    halo = halo_ref[...]
    glu_halo = halo[:, :CONV_CH] * _sigmoid(halo[:, CONV_CH:])
    buf_ref[0:CONV_HALO, :] = jnp.where(first, 0.0, glu_halo)
    a = a_ref[...]
    buf_ref[CONV_HALO:, :] = a[:, :CONV_CH] * _sigmoid(a[:, CONV_CH:])

    lead = CONV_HALO - (CONV_WIDTH - 1)

    for r0 in range(0, ROW_TILE, CONV_ROW_CHUNK):
        acc = jnp.broadcast_to(dwb_ref[...], (CONV_ROW_CHUNK, CONV_CH))
        for k in range(CONV_WIDTH):
            start = r0 + lead + k
            acc = acc + dww_ref[k:k + 1, :] * buf_ref[start:start + CONV_ROW_CHUNK, :]
        y = _layernorm(acc, cg_ref[...], cb_ref[...])
        o_ref[r0:r0 + CONV_ROW_CHUNK, 0:CONV_CH] = (y * _sigmoid(y)).astype(BF16)

    lane = lax.broadcasted_iota(jnp.int32, (1, GMLP_CH), 1)
    for ch in range(ROW_TILE // CHUNK):
        rows = slice(ch * CHUNK, (ch + 1) * CHUNK)
        c = c_ref[rows, :]
        vn = _layernorm(c[:, GMLP_CH:], gg_ref[...], gb_ref[...])
        stacked = jnp.concatenate(
            [jnp.where((lane // GMLP_GROUP_DIM) == g, vn, 0.0) for g in range(GMLP_GROUPS)],
            axis=0).astype(BF16)
        mixed = jnp.dot(ws_ref[...], stacked, preferred_element_type=F32) + bs_ref[...]
        o_ref[rows, CONV_CH:] = (c[:, :GMLP_CH] * mixed).astype(BF16)


def _conv_gmlp(h, seq, dww, dwb, cg, cb, gg, gb, ws_cat, bs_tab):
    n = h.shape[0]
    tiles_per_seq = seq // ROW_TILE
    halo_per_tile = ROW_TILE // CONV_HALO
    return pl.pallas_call(
        functools.partial(_conv_gmlp_body, tiles_per_seq),
        grid=(n // ROW_TILE,),
        in_specs=[pl.BlockSpec((ROW_TILE, COL_BLOCK), lambda i: (i, COL_A)),
                  pl.BlockSpec((CONV_HALO, COL_BLOCK),
                               lambda i: (jnp.maximum(i * halo_per_tile - 1, 0), COL_A)),
                  pl.BlockSpec((ROW_TILE, COL_BLOCK), lambda i: (i, COL_C)),
                  _const_spec(dww.shape), _const_spec(dwb.shape),
                  _const_spec(cg.shape), _const_spec(cb.shape),
                  _const_spec(gg.shape), _const_spec(gb.shape),
                  _const_spec(ws_cat.shape), _const_spec(bs_tab.shape)],
        out_specs=pl.BlockSpec((ROW_TILE, CONV_CH + GMLP_CH), lambda i: (i, 0)),
        out_shape=jax.ShapeDtypeStruct((n, CONV_CH + GMLP_CH), BF16),
        scratch_shapes=[pltpu.VMEM((CONV_HALO + ROW_TILE, CONV_CH), F32)],
        compiler_params=_params("parallel"),
        name="conv_gmlp",
    )(h, h, h, dww, dwb, cg, cb, gg, gb, ws_cat, bs_tab)


def _attn_scores(q2, k2, bias_lo, bias_hi):
    lo = lax.broadcasted_iota(jnp.int32, (1, HEAD_PAIR), 1) < HEAD_DIM
    nt = (((1,), (1,)), ((), ()))
    q2 = q2 * (HEAD_DIM ** -0.5 * LOG2_E)
    kb = k2.astype(BF16)
    ps, ms = [], []
    for keep, bias in ((lo, bias_lo), (jnp.logical_not(lo), bias_hi)):
        qh = jnp.where(keep, q2, 0.0).astype(BF16)
        s = lax.dot_general(qh, kb, nt, preferred_element_type=F32) + bias
        m = jnp.max(s, axis=-1, keepdims=True)
        ps.append(jnp.exp2(s - m).astype(BF16))
        ms.append(m)
    return jnp.concatenate(ps, axis=1), jnp.where(lo, ms[0], ms[1])


def _attn_values(p, v2):
    lo = lax.broadcasted_iota(jnp.int32, (1, HEAD_PAIR), 1) < HEAD_DIM
    shape = (2 * ATTN_BLOCK, HEAD_PAIR)
    ind_lo = jnp.broadcast_to(jnp.where(lo, 1.0, 0.0), shape).astype(BF16)
    ind_hi = jnp.broadcast_to(jnp.where(lo, 0.0, 1.0), shape).astype(BF16)
    vb = v2.astype(BF16)
    zero = jnp.zeros_like(vb)
    v_stack = jnp.concatenate(
        [jnp.concatenate([jnp.where(lo, vb, zero), ind_lo], axis=1),
         jnp.concatenate([jnp.where(lo, zero, vb), ind_hi], axis=1)], axis=0)
    ol = jnp.dot(p, v_stack, preferred_element_type=F32)
    return ol[:, :HEAD_PAIR], ol[:, HEAD_PAIR:]


def _attn_body(q_ref, kc_ref, kp_ref, vc_ref, vp_ref, bias_ref, out_ref,
               k_all, v_all, p_scr, o_scr, m_scr, l_scr):
    pair = pl.program_id(1)
    edge = jnp.where(pl.program_id(2) == 0, 0, 1)
    k_all[0:ATTN_TILE, :] = kp_ref[...]
    k_all[ATTN_TILE:, :] = kc_ref[...]
    v_all[0:ATTN_TILE, :] = vp_ref[...]
    v_all[ATTN_TILE:, :] = vc_ref[...]

    def rows(start, n, d):
        return pl.ds(start, n) if d == 1 else pl.ds(start, n, stride=d)

    units = []
    for p, (_, d) in enumerate(DILATED_PATTERNS):
        span = d * ATTN_BLOCK
        for r in range(d):
            for j in range(ATTN_TILE // span):
                units.append((p, d, r + j * span, span, j == 0))
    groups = [units[g:g + ATTN_GROUP] for g in range(0, len(units), ATTN_GROUP)]

    def scores(group, slot):
        for t, (p, d, start, span, seq_edge) in enumerate(group):
            sel = edge if seq_edge else 1
            keys = rows(ATTN_TILE + start - span, 2 * ATTN_BLOCK, d)
            probs, m = _attn_scores(q_ref[rows(start, ATTN_BLOCK, d), :], k_all[keys, :],
                                    bias_ref[p, sel, 2 * pair], bias_ref[p, sel, 2 * pair + 1])
            p_scr[slot, t] = probs
            m_scr[p, rows(start, ATTN_BLOCK, d), :] = m

    def values(group, slot):
        for t, (p, d, start, span, _) in enumerate(group):
            keys = rows(ATTN_TILE + start - span, 2 * ATTN_BLOCK, d)
            o, l = _attn_values(p_scr[slot, t], v_all[keys, :])
            o_scr[p, rows(start, ATTN_BLOCK, d), :] = o
            l_scr[p, rows(start, ATTN_BLOCK, d), :] = l

    scores(groups[0], 0)
    for g in range(1, len(groups)):
        scores(groups[g], g % 2)
        values(groups[g - 1], (g - 1) % 2)
    values(groups[-1], (len(groups) - 1) % 2)

    def merge(i, carry):
        r = pl.ds(pl.multiple_of(i * MERGE_ROWS, MERGE_ROWS), MERGE_ROWS)
        ms = [m_scr[p, r, :] for p in range(len(DILATED_PATTERNS))]
        big = functools.reduce(jnp.maximum, ms)
        ws = [jnp.exp2(m - big) for m in ms]
        num = sum(w * o_scr[p, r, :] for p, w in enumerate(ws))
        den = sum(w * l_scr[p, r, :] for p, w in enumerate(ws))
        out_ref[r, :] = (num / den).astype(BF16)
        return carry

    lax.fori_loop(0, ATTN_TILE // MERGE_ROWS, merge, 0)


def _attention(h, batch, seq, bias):
    n = h.shape[0]
    tiles = seq // ATTN_TILE
    lane_blocks = COL_BLOCK // LANES

    def cur(col):
        return pl.BlockSpec((ATTN_TILE, LANES),
                            lambda b, hp, t: (b * tiles + t, col * lane_blocks + hp))

    def prev(col):
        return pl.BlockSpec((ATTN_TILE, LANES),
                            lambda b, hp, t: (b * tiles + jnp.maximum(t - 1, 0),
                                              col * lane_blocks + hp))

    return pl.pallas_call(
        _attn_body,
        grid=(batch, N_PAIRS, tiles),
        in_specs=[cur(COL_Q), cur(COL_K), prev(COL_K), cur(COL_V), prev(COL_V),
                  _const_spec(bias.shape)],
        out_specs=pl.BlockSpec((ATTN_TILE, LANES), lambda b, hp, t: (b * tiles + t, hp)),
        out_shape=jax.ShapeDtypeStruct((n, ATTN_CH), BF16),
        scratch_shapes=[pltpu.VMEM((2 * ATTN_TILE, LANES), F32),
                        pltpu.VMEM((2 * ATTN_TILE, LANES), F32),
                        pltpu.VMEM((2, ATTN_GROUP, ATTN_BLOCK, 4 * ATTN_BLOCK), BF16),
                        pltpu.VMEM((len(DILATED_PATTERNS), ATTN_TILE, LANES), F32),
                        pltpu.VMEM((len(DILATED_PATTERNS), ATTN_TILE, LANES), F32),
                        pltpu.VMEM((len(DILATED_PATTERNS), ATTN_TILE, LANES), F32)],
        compiler_params=_params("parallel", "parallel", "arbitrary"),
        name="dilated_attn",
    )(h, h, h, h, h, bias)


def _t5_bucket(dist):
    max_exact = N_BUCKETS // 2
    d = np.maximum(dist, 1).astype(np.float64)
    large = max_exact + (np.log(d / max_exact) / math.log(MAX_DISTANCE / max_exact)
                         * (N_BUCKETS - max_exact)).astype(np.int32)
    large = np.minimum(large, N_BUCKETS - 1)
    return np.where(dist < max_exact, dist, large).astype(np.int32)


def _attn_bias(rel_table, window, dilation):
    n_win = window // dilation
    assert n_win <= ATTN_BLOCK
    heads = rel_table.shape[1]
    onehot = np.eye(N_BUCKETS, dtype=np.float32)[_t5_bucket(np.arange(n_win + 1) * dilation)]
    by_dist = jnp.sum(onehot[:, :, None] * rel_table.astype(F32)[None], axis=1)
    period = 3 * ATTN_BLOCK + 1
    u = jnp.concatenate([jnp.full((heads, ATTN_BLOCK - n_win), NEG_LOGIT, F32), by_dist[::-1].T,
                         jnp.full((heads, period - ATTN_BLOCK - 1), NEG_LOGIT, F32)], axis=1)
    toeplitz = jnp.tile(u, (1, ATTN_BLOCK))[:, :ATTN_BLOCK * (period - 1)]
    main = toeplitz.reshape(heads, ATTN_BLOCK, period - 1)[:, :, :2 * ATTN_BLOCK]
    kj = np.arange(2 * ATTN_BLOCK)[None, None, :]
    first = jnp.where(jnp.asarray(kj >= ATTN_BLOCK), main, NEG_LOGIT)
    return jnp.stack([first, main]) * LOG2_E


def _outproj_body(alpha, ac_ref, attn_ref, x_ref, w_ref, b_ref, g_ref, beta_ref, y_ref):
    half = CONV_CH + GMLP_CH
    mix = (jnp.dot(ac_ref[...], w_ref[0:half, :], preferred_element_type=F32)
           + jnp.dot(attn_ref[...], w_ref[half:, :], preferred_element_type=F32) + b_ref[...])
    y_ref[...] = _layernorm(alpha * x_ref[...] + mix, g_ref[...], beta_ref[...])


def _outproj(alpha, ac, attn, x, w, b, g, beta):
    n = x.shape[0]
    row = lambda width: pl.BlockSpec((ROW_TILE, width), lambda i: (i, 0))
    return pl.pallas_call(
        functools.partial(_outproj_body, alpha),
        grid=(n // ROW_TILE,),
        in_specs=[row(CONV_CH + GMLP_CH), row(ATTN_CH), row(D_MODEL),
                  _const_spec(w.shape), _const_spec(b.shape), _const_spec(g.shape),
                  _const_spec(beta.shape)],
        out_specs=row(D_MODEL),
        out_shape=jax.ShapeDtypeStruct((n, D_MODEL), F32),
        compiler_params=_params("parallel"),
        name="outproj_ln",
    )(ac, attn, x, w, b, g, beta)


def _ffn_body(alpha, tiles_per_seq, x_ref, wup_ref, bup_ref, cw_ref, cb_ref, wdn_ref, bdn_ref,
              g_ref, beta_ref, y_ref, xb_ref, h_buf, act_ref, tail_ref):
    first = (pl.program_id(0) % tiles_per_seq) == 0
    x = x_ref[...]
    xb_ref[...] = x.astype(BF16)
    acc = alpha * x + bdn_ref[...]
    width = 2 * FF_CHUNK
    slabs = width // LANES
    n_chunks = D_FF // FF_CHUNK

    def up(c):
        cols = slice(c * width, (c + 1) * width)
        h = jnp.dot(xb_ref[...], wup_ref[:, cols], preferred_element_type=F32) + bup_ref[:, cols]
        halo = jnp.where(first, 0.0, tail_ref[:, cols])
        tail_ref[:, cols] = h[ROW_TILE - FFN_HALO:, :]
        for s in range(slabs):
            lanes = slice(s * LANES, (s + 1) * LANES)
            buf = h_buf.at[c % FFN_SLOTS, s]
            buf[pl.ds(0, FFN_HALO, stride=2), :] = halo[:, lanes]
            buf[pl.ds(2 * FFN_HALO, ROW_TILE, stride=2), :] = h[:, lanes]

    def conv(c, s, r0):
        wcols = slice(c * width + s * LANES, c * width + (s + 1) * LANES)
        y = cb_ref[:, wcols]
        for k in range(FFN_CONV_WIDTH):
            shift = FFN_CONV_WIDTH - 1 - k
            y = y + cw_ref[k:k + 1, wcols] * h_buf[c % FFN_SLOTS, s, pl.ds(
                2 * (FFN_HALO + r0 - shift), FFN_ROWS, stride=2), :]
        return y

    for c in range(FFN_LOOKAHEAD):
        up(c)
    for c in range(n_chunks):
        if c + FFN_LOOKAHEAD < n_chunks:
            up(c + FFN_LOOKAHEAD)
        for r0 in range(0, ROW_TILE, FFN_ROWS):
            for s in range(slabs // 2):
                gate, val = conv(c, s, r0), conv(c, s + slabs // 2, r0)
                act_ref[c % 2, r0:r0 + FFN_ROWS, s * LANES:(s + 1) * LANES] = (
                    gate * _sigmoid(gate) * val).astype(BF16)
        acc = acc + jnp.dot(act_ref[c % 2], wdn_ref[c * FF_CHUNK:(c + 1) * FF_CHUNK, :],
                            preferred_element_type=F32)
    y_ref[...] = _layernorm(acc, g_ref[...], beta_ref[...])


def _ffn(alpha, x, seq, wup, bup, cw, cb, wdn, bdn, g, beta):
    n = x.shape[0]
    row = pl.BlockSpec((ROW_TILE, D_MODEL), lambda i: (i, 0))
    consts = [wup, bup, cw, cb, wdn, bdn, g, beta]
    return pl.pallas_call(
        functools.partial(_ffn_body, alpha, seq // ROW_TILE),
        grid=(n // ROW_TILE,),
        in_specs=[row] + [_const_spec(c.shape) for c in consts],
        out_specs=row,
        out_shape=jax.ShapeDtypeStruct((n, D_MODEL), F32),
        scratch_shapes=[pltpu.VMEM((ROW_TILE, D_MODEL), BF16),
                        pltpu.VMEM((FFN_SLOTS, 2 * FF_CHUNK // LANES, 2 * (FFN_HALO + ROW_TILE),
                                    LANES), F32),
                        pltpu.VMEM((2, ROW_TILE, FF_CHUNK), BF16),
                        pltpu.VMEM((FFN_HALO, 2 * D_FF), F32)],
        compiler_params=_params("arbitrary"),
        name="convffn_ln",
    )(x, *consts)


def _ffn_interleave(v):
    lead = v.shape[:-1]
    v = v.reshape(lead + (2, D_FF // FF_CHUNK, FF_CHUNK))
    return jnp.swapaxes(v, -3, -2).reshape(lead + (2 * D_FF,))


def kernel(x, w_in, b_in, conv_dw_w, conv_dw_b, conv_ln_g, conv_ln_b, rel_bias_table, gmlp_ln_g, gmlp_ln_b, gmlp_w_s, gmlp_b_s, w_out, b_out, ln1_g, ln1_b, ffn_w_up, ffn_b_up, ffn_conv_w, ffn_conv_b, ffn_w_down, ffn_b_down, ln2_g, ln2_b):
    batch, seq, _ = x.shape
    depth = w_in.shape[0]
    alpha = (2.0 * depth) ** 0.25
    n = batch * seq
    row2d = lambda v: v.reshape(1, -1)
    bias = jnp.stack([_attn_bias(rel_bias_table, w, d) for (w, d) in DILATED_PATTERNS])

    xf = x.reshape(n, D_MODEL)
    for l in range(depth):
        h = _inproj(xf, w_in[l].astype(BF16), row2d(b_in[l]))
        ws_cat = jnp.transpose(jnp.tril(gmlp_w_s[l]), (1, 0, 2)).reshape(CHUNK, GMLP_GROUPS * CHUNK)
        bs_tab = jnp.repeat(gmlp_b_s[l].T, GMLP_GROUP_DIM, axis=1)
        ac = _conv_gmlp(h, seq, conv_dw_w[l], row2d(conv_dw_b[l]), row2d(conv_ln_g[l]),
                        row2d(conv_ln_b[l]), row2d(gmlp_ln_g[l]), row2d(gmlp_ln_b[l]),
                        ws_cat.astype(BF16), bs_tab)
        attn = _attention(h, batch, seq, bias)
        wo = w_out[l].astype(BF16)
        wo = jnp.concatenate([wo[:CONV_CH], wo[CONV_CH + ATTN_CH:], wo[CONV_CH:CONV_CH + ATTN_CH]])
        x1 = _outproj(alpha, ac, attn, xf, wo, row2d(b_out[l]), row2d(ln1_g[l]), row2d(ln1_b[l]))
        xf = _ffn(alpha, x1, seq, _ffn_interleave(ffn_w_up[l].astype(BF16)),
                  row2d(_ffn_interleave(ffn_b_up[l])), _ffn_interleave(ffn_conv_w[l]),
                  row2d(_ffn_interleave(ffn_conv_b[l])), ffn_w_down[l].astype(BF16),
                  row2d(ffn_b_down[l]), row2d(ln2_g[l]), row2d(ln2_b[l]))
    return xf.reshape(batch, seq, D_MODEL)
```

```python
import functools
import math

import jax
import jax.numpy as jnp
import numpy as np
from jax import lax
from jax.experimental import pallas as pl
from jax.experimental.pallas import tpu as pltpu

D_MODEL = 1024
HEAD_DIM = 64
CONV_CH = 256
CONV_WIDTH = 31
ATTN_HEADS = 8
ATTN_CH = ATTN_HEADS * HEAD_DIM
DILATED_PATTERNS = ((128, 1), (512, 4), (2048, 16))
ATTN_BLOCK = 128
N_BUCKETS = 32
MAX_DISTANCE = 2048
GMLP_CH = 256
GMLP_GROUPS = 4
GMLP_GROUP_DIM = GMLP_CH // GMLP_GROUPS
CHUNK = 128
MIX_CH = CONV_CH + ATTN_CH + GMLP_CH
IN_CH = 2 * CONV_CH + 3 * ATTN_CH + 2 * GMLP_CH
D_FF = 2816
FFN_CONV_WIDTH = 3
LN_EPS = 1e-5

LANES = 128
COL_BLOCK = 512
H_BLOCKS = IN_CH // COL_BLOCK
COL_A, COL_Q, COL_K, COL_V, COL_C = range(5)

ROW_TILE = 512
CONV_HALO = 32
CONV_ROW_CHUNK = 64
HEAD_PAIR = 2 * HEAD_DIM
N_PAIRS = ATTN_HEADS // 2
ATTN_TILE = ATTN_BLOCK * max(d for _, d in DILATED_PATTERNS)
ATTN_GROUP = 4
MERGE_ROWS = 256
FF_CHUNK = 256
FFN_LOOKAHEAD = 1
FFN_SLOTS = FFN_LOOKAHEAD + 1
FFN_ROWS = 64
FFN_HALO = 8
LOG2_E = math.log2(math.e)
NEG_LOGIT = -1e30
VMEM_LIMIT_BYTES = 56 * 1024 * 1024

BF16 = jnp.bfloat16
F32 = jnp.float32


def _const_spec(shape):
    zeros = (0,) * len(shape)
    return pl.BlockSpec(shape, lambda *_: zeros, pipeline_mode=pl.Buffered(1))


def _params(*semantics):
    return pltpu.CompilerParams(dimension_semantics=semantics,
                                vmem_limit_bytes=VMEM_LIMIT_BYTES)


def _layernorm(x, g, b):
    mu = jnp.mean(x, axis=-1, keepdims=True)
    xc = x - mu
    var = jnp.mean(xc * xc, axis=-1, keepdims=True)
    return xc * lax.rsqrt(var + LN_EPS) * g + b


def _sigmoid(x):
    return 1.0 / (1.0 + jnp.exp(-x))


def _inproj_body(x_ref, w_ref, b_ref, h_ref):
    xb = x_ref[...].astype(BF16)
    for j in range(H_BLOCKS):
        cols = slice(j * COL_BLOCK, (j + 1) * COL_BLOCK)
        h_ref[:, cols] = (jnp.dot(xb, w_ref[:, cols], preferred_element_type=F32)
                          + b_ref[:, cols])


def _inproj(x, w, b):
    n = x.shape[0]
    return pl.pallas_call(
        _inproj_body,
        grid=(n // ROW_TILE,),
        in_specs=[pl.BlockSpec((ROW_TILE, D_MODEL), lambda i: (i, 0)),
                  _const_spec((D_MODEL, IN_CH)),
                  _const_spec((1, IN_CH))],
        out_specs=pl.BlockSpec((ROW_TILE, IN_CH), lambda i: (i, 0)),
        out_shape=jax.ShapeDtypeStruct((n, IN_CH), F32),
        compiler_params=_params("parallel"),
        name="inproj",
    )(x, w, b)


def _conv_gmlp_body(tiles_per_seq, a_ref, halo_ref, c_ref, dww_ref, dwb_ref, cg_ref, cb_ref,
                    gg_ref, gb_ref, ws_ref, bs_ref, o_ref, buf_ref):
    first = (pl.program_id(0) % tiles_per_seq) == 0

    halo = halo_ref[...]
    glu_halo = jnp.where(first, 0.0, halo[:, :CONV_CH] * _sigmoid(halo[:, CONV_CH:]))
    a = a_ref[...]
    glu = a[:, :CONV_CH] * _sigmoid(a[:, CONV_CH:])
    slabs = CONV_CH // LANES
    for s in range(slabs):
        lanes = slice(s * LANES, (s + 1) * LANES)
        buf_ref[s, pl.ds(0, CONV_HALO, stride=2), :] = glu_halo[:, lanes]
        buf_ref[s, pl.ds(2 * CONV_HALO, ROW_TILE, stride=2), :] = glu[:, lanes]

    lead = CONV_HALO - (CONV_WIDTH - 1)

    for r0 in range(0, ROW_TILE, CONV_ROW_CHUNK):
        accs = []
        for s in range(slabs):
            lanes = slice(s * LANES, (s + 1) * LANES)
            acc = jnp.broadcast_to(dwb_ref[:, lanes], (CONV_ROW_CHUNK, LANES))
            for k in range(CONV_WIDTH):
                window = pl.ds(2 * (r0 + lead + k), CONV_ROW_CHUNK, stride=2)
                acc = acc + dww_ref[k:k + 1, lanes] * buf_ref[s, window, :]
            accs.append(acc)
        y = _layernorm(jnp.concatenate(accs, axis=1), cg_ref[...], cb_ref[...])
        o_ref[r0:r0 + CONV_ROW_CHUNK, 0:CONV_CH] = (y * _sigmoid(y)).astype(BF16)

    lane = lax.broadcasted_iota(jnp.int32, (1, GMLP_CH), 1)
    for ch in range(ROW_TILE // CHUNK):
        rows = slice(ch * CHUNK, (ch + 1) * CHUNK)
        c = c_ref[rows, :]
        vn = _layernorm(c[:, GMLP_CH:], gg_ref[...], gb_ref[...])
        stacked = jnp.concatenate(
            [jnp.where((lane // GMLP_GROUP_DIM) == g, vn, 0.0) for g in range(GMLP_GROUPS)],
            axis=0).astype(BF16)
        mixed = jnp.dot(ws_ref[...], stacked, preferred_element_type=F32) + bs_ref[...]
        o_ref[rows, CONV_CH:] = (c[:, :GMLP_CH] * mixed).astype(BF16)


def _conv_gmlp(h, seq, dww, dwb, cg, cb, gg, gb, ws_cat, bs_tab):
    n = h.shape[0]
    tiles_per_seq = seq // ROW_TILE
    halo_per_tile = ROW_TILE // CONV_HALO
    return pl.pallas_call(
        functools.partial(_conv_gmlp_body, tiles_per_seq),
        grid=(n // ROW_TILE,),
        in_specs=[pl.BlockSpec((ROW_TILE, COL_BLOCK), lambda i: (i, COL_A)),
                  pl.BlockSpec((CONV_HALO, COL_BLOCK),
                               lambda i: (jnp.maximum(i * halo_per_tile - 1, 0), COL_A)),
                  pl.BlockSpec((ROW_TILE, COL_BLOCK), lambda i: (i, COL_C)),
                  _const_spec(dww.shape), _const_spec(dwb.shape),
                  _const_spec(cg.shape), _const_spec(cb.shape),
                  _const_spec(gg.shape), _const_spec(gb.shape),
                  _const_spec(ws_cat.shape), _const_spec(bs_tab.shape)],
        out_specs=pl.BlockSpec((ROW_TILE, CONV_CH + GMLP_CH), lambda i: (i, 0)),
        out_shape=jax.ShapeDtypeStruct((n, CONV_CH + GMLP_CH), BF16),
        scratch_shapes=[pltpu.VMEM((CONV_CH // LANES, 2 * (CONV_HALO + ROW_TILE), LANES), F32)],
        compiler_params=_params("parallel"),
        name="conv_gmlp",
    )(h, h, h, dww, dwb, cg, cb, gg, gb, ws_cat, bs_tab)


def _attn_scores(q2, k2, bias_lo, bias_hi):
    lo = lax.broadcasted_iota(jnp.int32, (1, HEAD_PAIR), 1) < HEAD_DIM
    nt = (((1,), (1,)), ((), ()))
    q2 = q2 * (HEAD_DIM ** -0.5 * LOG2_E)
    kb = k2.astype(BF16)
    ps, ms = [], []
    for keep, bias in ((lo, bias_lo), (jnp.logical_not(lo), bias_hi)):
        qh = jnp.where(keep, q2, 0.0).astype(BF16)
        s = lax.dot_general(qh, kb, nt, preferred_element_type=F32) + bias
        m = jnp.max(s, axis=-1, keepdims=True)
        ps.append(jnp.exp2(s - m).astype(BF16))
        ms.append(m)
    return jnp.concatenate(ps, axis=1), jnp.where(lo, ms[0], ms[1])


def _attn_values(p, v2):
    lo = lax.broadcasted_iota(jnp.int32, (1, HEAD_PAIR), 1) < HEAD_DIM
    shape = (2 * ATTN_BLOCK, HEAD_PAIR)
    ind_lo = jnp.broadcast_to(jnp.where(lo, 1.0, 0.0), shape).astype(BF16)
    ind_hi = jnp.broadcast_to(jnp.where(lo, 0.0, 1.0), shape).astype(BF16)
    vb = v2.astype(BF16)
    zero = jnp.zeros_like(vb)
    v_stack = jnp.concatenate(
        [jnp.concatenate([jnp.where(lo, vb, zero), ind_lo], axis=1),
         jnp.concatenate([jnp.where(lo, zero, vb), ind_hi], axis=1)], axis=0)
    ol = jnp.dot(p, v_stack, preferred_element_type=F32)
    return ol[:, :HEAD_PAIR], ol[:, HEAD_PAIR:]


def _attn_body(q_ref, kc_ref, kp_ref, vc_ref, vp_ref, bias_ref, out_ref,
               k_all, v_all, p_scr, o_scr, m_scr, l_scr):
    pair = pl.program_id(1)
    edge = jnp.where(pl.program_id(2) == 0, 0, 1)
    k_all[0:ATTN_TILE, :] = kp_ref[...]
    k_all[ATTN_TILE:, :] = kc_ref[...]
    v_all[0:ATTN_TILE, :] = vp_ref[...]
    v_all[ATTN_TILE:, :] = vc_ref[...]

    def rows(start, n, d):
        return pl.ds(start, n) if d == 1 else pl.ds(start, n, stride=d)

    units = []
    for p, (_, d) in enumerate(DILATED_PATTERNS):
        span = d * ATTN_BLOCK
        for r in range(d):
            for j in range(ATTN_TILE // span):
                units.append((p, d, r + j * span, span, j == 0))
    groups = [units[g:g + ATTN_GROUP] for g in range(0, len(units), ATTN_GROUP)]

    def scores(group, slot):
        for t, (p, d, start, span, seq_edge) in enumerate(group):
            sel = edge if seq_edge else 1
            keys = rows(ATTN_TILE + start - span, 2 * ATTN_BLOCK, d)
            probs, m = _attn_scores(q_ref[rows(start, ATTN_BLOCK, d), :], k_all[keys, :],
                                    bias_ref[p, sel, 2 * pair], bias_ref[p, sel, 2 * pair + 1])
            p_scr[slot, t] = probs
            m_scr[p, rows(start, ATTN_BLOCK, d), :] = m

    def values(group, slot):
        for t, (p, d, start, span, _) in enumerate(group):
            keys = rows(ATTN_TILE + start - span, 2 * ATTN_BLOCK, d)
            o, l = _attn_values(p_scr[slot, t], v_all[keys, :])
            o_scr[p, rows(start, ATTN_BLOCK, d), :] = o
            l_scr[p, rows(start, ATTN_BLOCK, d), :] = l

    scores(groups[0], 0)
    for g in range(1, len(groups)):
        scores(groups[g], g % 2)
        values(groups[g - 1], (g - 1) % 2)
    values(groups[-1], (len(groups) - 1) % 2)

    def merge(i, carry):
        r = pl.ds(pl.multiple_of(i * MERGE_ROWS, MERGE_ROWS), MERGE_ROWS)
        ms = [m_scr[p, r, :] for p in range(len(DILATED_PATTERNS))]
        big = functools.reduce(jnp.maximum, ms)
        ws = [jnp.exp2(m - big) for m in ms]
        num = sum(w * o_scr[p, r, :] for p, w in enumerate(ws))
        den = sum(w * l_scr[p, r, :] for p, w in enumerate(ws))
        out_ref[r, :] = (num / den).astype(BF16)
        return carry

    lax.fori_loop(0, ATTN_TILE // MERGE_ROWS, merge, 0)


def _attention(h, batch, seq, bias):
    n = h.shape[0]
    tiles = seq // ATTN_TILE
    lane_blocks = COL_BLOCK // LANES

    def cur(col):
        return pl.BlockSpec((ATTN_TILE, LANES),
                            lambda b, hp, t: (b * tiles + t, col * lane_blocks + hp))

    def prev(col):
        return pl.BlockSpec((ATTN_TILE, LANES),
                            lambda b, hp, t: (b * tiles + jnp.maximum(t - 1, 0),
                                              col * lane_blocks + hp))

    return pl.pallas_call(
        _attn_body,
        grid=(batch, N_PAIRS, tiles),
        in_specs=[cur(COL_Q), cur(COL_K), prev(COL_K), cur(COL_V), prev(COL_V),
                  _const_spec(bias.shape)],
        out_specs=pl.BlockSpec((ATTN_TILE, LANES), lambda b, hp, t: (b * tiles + t, hp)),
        out_shape=jax.ShapeDtypeStruct((n, ATTN_CH), BF16),
        scratch_shapes=[pltpu.VMEM((2 * ATTN_TILE, LANES), F32),
                        pltpu.VMEM((2 * ATTN_TILE, LANES), F32),
                        pltpu.VMEM((2, ATTN_GROUP, ATTN_BLOCK, 4 * ATTN_BLOCK), BF16),
                        pltpu.VMEM((len(DILATED_PATTERNS), ATTN_TILE, LANES), F32),
                        pltpu.VMEM((len(DILATED_PATTERNS), ATTN_TILE, LANES), F32),
                        pltpu.VMEM((len(DILATED_PATTERNS), ATTN_TILE, LANES), F32)],
        compiler_params=_params("parallel", "parallel", "arbitrary"),
        name="dilated_attn",
    )(h, h, h, h, h, bias)


def _t5_bucket(dist):
    max_exact = N_BUCKETS // 2
    d = np.maximum(dist, 1).astype(np.float64)
    large = max_exact + (np.log(d / max_exact) / math.log(MAX_DISTANCE / max_exact)
                         * (N_BUCKETS - max_exact)).astype(np.int32)
    large = np.minimum(large, N_BUCKETS - 1)
    return np.where(dist < max_exact, dist, large).astype(np.int32)


def _attn_bias(rel_table, window, dilation):
    n_win = window // dilation
    assert n_win <= ATTN_BLOCK
    heads = rel_table.shape[1]
    onehot = np.eye(N_BUCKETS, dtype=np.float32)[_t5_bucket(np.arange(n_win + 1) * dilation)]
    by_dist = jnp.sum(onehot[:, :, None] * rel_table.astype(F32)[None], axis=1)
    period = 3 * ATTN_BLOCK + 1
    u = jnp.concatenate([jnp.full((heads, ATTN_BLOCK - n_win), NEG_LOGIT, F32), by_dist[::-1].T,
                         jnp.full((heads, period - ATTN_BLOCK - 1), NEG_LOGIT, F32)], axis=1)
    toeplitz = jnp.tile(u, (1, ATTN_BLOCK))[:, :ATTN_BLOCK * (period - 1)]
    main = toeplitz.reshape(heads, ATTN_BLOCK, period - 1)[:, :, :2 * ATTN_BLOCK]
    kj = np.arange(2 * ATTN_BLOCK)[None, None, :]
    first = jnp.where(jnp.asarray(kj >= ATTN_BLOCK), main, NEG_LOGIT)
    return jnp.stack([first, main]) * LOG2_E


def _outproj_body(alpha, ac_ref, attn_ref, x_ref, w_ref, b_ref, g_ref, beta_ref, y_ref):
    half = CONV_CH + GMLP_CH
    mix = (jnp.dot(ac_ref[...], w_ref[0:half, :], preferred_element_type=F32)
           + jnp.dot(attn_ref[...], w_ref[half:, :], preferred_element_type=F32) + b_ref[...])
    y_ref[...] = _layernorm(alpha * x_ref[...] + mix, g_ref[...], beta_ref[...])


def _outproj(alpha, ac, attn, x, w, b, g, beta):
    n = x.shape[0]
    row = lambda width: pl.BlockSpec((ROW_TILE, width), lambda i: (i, 0))
    return pl.pallas_call(
        functools.partial(_outproj_body, alpha),
        grid=(n // ROW_TILE,),
        in_specs=[row(CONV_CH + GMLP_CH), row(ATTN_CH), row(D_MODEL),
                  _const_spec(w.shape), _const_spec(b.shape), _const_spec(g.shape),
                  _const_spec(beta.shape)],
        out_specs=row(D_MODEL),
        out_shape=jax.ShapeDtypeStruct((n, D_MODEL), F32),
        compiler_params=_params("parallel"),
        name="outproj_ln",
    )(ac, attn, x, w, b, g, beta)


def _ffn_body(alpha, tiles_per_seq, x_ref, wup_ref, bup_ref, cw_ref, cb_ref, wdn_ref, bdn_ref,
              g_ref, beta_ref, y_ref, xb_ref, h_buf, act_ref, tail_ref):
    first = (pl.program_id(0) % tiles_per_seq) == 0
    x = x_ref[...]
    xb_ref[...] = x.astype(BF16)
    acc = alpha * x + bdn_ref[...]
    half_slabs = FF_CHUNK // LANES
    slabs = 2 * half_slabs
    n_chunks = D_FF // FF_CHUNK

    def chunk_cols(c, half):
        start = half * D_FF + c * FF_CHUNK
        return slice(start, start + FF_CHUNK)

    def up(c):
        for half in range(2):
            cols = chunk_cols(c, half)
            h = (jnp.dot(xb_ref[...], wup_ref[:, cols], preferred_element_type=F32)
                 + bup_ref[:, cols])
            halo = jnp.where(first, 0.0, tail_ref[:, cols])
            tail_ref[:, cols] = h[ROW_TILE - FFN_HALO:, :]
            for s in range(half_slabs):
                lanes = slice(s * LANES, (s + 1) * LANES)
                buf = h_buf.at[c % FFN_SLOTS, half * half_slabs + s]
                buf[pl.ds(0, FFN_HALO, stride=2), :] = halo[:, lanes]
                buf[pl.ds(2 * FFN_HALO, ROW_TILE, stride=2), :] = h[:, lanes]

    def conv(c, s, r0):
        start = chunk_cols(c, s // half_slabs).start + (s % half_slabs) * LANES
        wcols = slice(start, start + LANES)
        y = cb_ref[:, wcols]
        for k in range(FFN_CONV_WIDTH):
            shift = FFN_CONV_WIDTH - 1 - k
            y = y + cw_ref[k:k + 1, wcols] * h_buf[c % FFN_SLOTS, s, pl.ds(
                2 * (FFN_HALO + r0 - shift), FFN_ROWS, stride=2), :]
        return y

    for c in range(FFN_LOOKAHEAD):
        up(c)
    for c in range(n_chunks):
        if c + FFN_LOOKAHEAD < n_chunks:
            up(c + FFN_LOOKAHEAD)
        for r0 in range(0, ROW_TILE, FFN_ROWS):
            for s in range(half_slabs):
                gate, val = conv(c, s, r0), conv(c, s + half_slabs, r0)
                act_ref[c % 2, r0:r0 + FFN_ROWS, s * LANES:(s + 1) * LANES] = (
                    gate * _sigmoid(gate) * val).astype(BF16)
        acc = acc + jnp.dot(act_ref[c % 2], wdn_ref[c * FF_CHUNK:(c + 1) * FF_CHUNK, :],
                            preferred_element_type=F32)
    y_ref[...] = _layernorm(acc, g_ref[...], beta_ref[...])


def _ffn(alpha, x, seq, wup, bup, cw, cb, wdn, bdn, g, beta):
    n = x.shape[0]
    row = pl.BlockSpec((ROW_TILE, D_MODEL), lambda i: (i, 0))
    consts = [wup, bup, cw, cb, wdn, bdn, g, beta]
    return pl.pallas_call(
        functools.partial(_ffn_body, alpha, seq // ROW_TILE),
        grid=(n // ROW_TILE,),
        in_specs=[row] + [_const_spec(c.shape) for c in consts],
        out_specs=row,
        out_shape=jax.ShapeDtypeStruct((n, D_MODEL), F32),
        scratch_shapes=[pltpu.VMEM((ROW_TILE, D_MODEL), BF16),
                        pltpu.VMEM((FFN_SLOTS, 2 * FF_CHUNK // LANES, 2 * (FFN_HALO + ROW_TILE),
                                    LANES), F32),
                        pltpu.VMEM((2, ROW_TILE, FF_CHUNK), BF16),
                        pltpu.VMEM((FFN_HALO, 2 * D_FF), F32)],
        compiler_params=_params("arbitrary"),
        name="convffn_ln",
    )(x, *consts)


def kernel(x, w_in, b_in, conv_dw_w, conv_dw_b, conv_ln_g, conv_ln_b, rel_bias_table, gmlp_ln_g, gmlp_ln_b, gmlp_w_s, gmlp_b_s, w_out, b_out, ln1_g, ln1_b, ffn_w_up, ffn_b_up, ffn_conv_w, ffn_conv_b, ffn_w_down, ffn_b_down, ln2_g, ln2_b):
    batch, seq, _ = x.shape
    depth = w_in.shape[0]
    alpha = (2.0 * depth) ** 0.25
    n = batch * seq
    row2d = lambda v: v.reshape(1, -1)
    bias = jnp.stack([_attn_bias(rel_bias_table, w, d) for (w, d) in DILATED_PATTERNS])

    xf = x.reshape(n, D_MODEL)
    for l in range(depth):
        h = _inproj(xf, w_in[l].astype(BF16), row2d(b_in[l]))
        ws_cat = jnp.transpose(jnp.tril(gmlp_w_s[l]), (1, 0, 2)).reshape(CHUNK, GMLP_GROUPS * CHUNK)
        bs_tab = jnp.repeat(gmlp_b_s[l].T, GMLP_GROUP_DIM, axis=1)
        ac = _conv_gmlp(h, seq, conv_dw_w[l], row2d(conv_dw_b[l]), row2d(conv_ln_g[l]),
                        row2d(conv_ln_b[l]), row2d(gmlp_ln_g[l]), row2d(gmlp_ln_b[l]),
                        ws_cat.astype(BF16), bs_tab)
        attn = _attention(h, batch, seq, bias)
        wo = w_out[l].astype(BF16)
        wo = jnp.concatenate([wo[:CONV_CH], wo[CONV_CH + ATTN_CH:], wo[CONV_CH:CONV_CH + ATTN_CH]])
        x1 = _outproj(alpha, ac, attn, xf, wo, row2d(b_out[l]), row2d(ln1_g[l]), row2d(ln1_b[l]))
        xf = _ffn(alpha, x1, seq, ffn_w_up[l].astype(BF16), row2d(ffn_b_up[l]), ffn_conv_w[l],
                  row2d(ffn_conv_b[l]), ffn_w_down[l].astype(BF16),
                  row2d(ffn_b_down[l]), row2d(ln2_g[l]), row2d(ln2_b[l]))
    return xf.reshape(batch, seq, D_MODEL)
```

```python
import functools
import math

import jax
import jax.numpy as jnp
import numpy as np
from jax import lax
from jax.experimental import pallas as pl
from jax.experimental.pallas import tpu as pltpu

D_MODEL = 1024
HEAD_DIM = 64
CONV_CH = 256
CONV_WIDTH = 31
ATTN_HEADS = 8
ATTN_CH = ATTN_HEADS * HEAD_DIM
DILATED_PATTERNS = ((128, 1), (512, 4), (2048, 16))
ATTN_BLOCK = 128
N_BUCKETS = 32
MAX_DISTANCE = 2048
GMLP_CH = 256
GMLP_GROUPS = 4
GMLP_GROUP_DIM = GMLP_CH // GMLP_GROUPS
CHUNK = 128
MIX_CH = CONV_CH + ATTN_CH + GMLP_CH
IN_CH = 2 * CONV_CH + 3 * ATTN_CH + 2 * GMLP_CH
D_FF = 2816
FFN_CONV_WIDTH = 3
LN_EPS = 1e-5

LANES = 128
COL_BLOCK = 512
H_BLOCKS = IN_CH // COL_BLOCK
COL_A, COL_Q, COL_K, COL_V, COL_C = range(5)

ROW_TILE = 512
CONV_HALO = 32
CONV_ROW_CHUNK = 64
HEAD_PAIR = 2 * HEAD_DIM
N_PAIRS = ATTN_HEADS // 2
ATTN_TILE = ATTN_BLOCK * max(d for _, d in DILATED_PATTERNS)
ATTN_GROUP = 4
MERGE_ROWS = 256
FF_CHUNK = 256
FFN_LOOKAHEAD = 1
FFN_SLOTS = FFN_LOOKAHEAD + 1
FFN_ROWS = 64
POST_TILES = 1
FFN_HALO = 8
LOG2_E = math.log2(math.e)
NEG_LOGIT = -1e30
VMEM_LIMIT_BYTES = 56 * 1024 * 1024

BF16 = jnp.bfloat16
F32 = jnp.float32


def _const_spec(shape):
    zeros = (0,) * len(shape)
    return pl.BlockSpec(shape, lambda *_: zeros, pipeline_mode=pl.Buffered(1))


def _params(*semantics):
    return pltpu.CompilerParams(dimension_semantics=semantics,
                                vmem_limit_bytes=VMEM_LIMIT_BYTES)


def _layernorm(x, g, b):
    mu = jnp.mean(x, axis=-1, keepdims=True)
    xc = x - mu
    var = jnp.mean(xc * xc, axis=-1, keepdims=True)
    return xc * lax.rsqrt(var + LN_EPS) * g + b


def _sigmoid(x):
    return 1.0 / (1.0 + jnp.exp(-x))


def _inproj_body(x_ref, w_ref, b_ref, h_ref):
    xb = x_ref[...].astype(BF16)
    for j in range(H_BLOCKS):
        cols = slice(j * COL_BLOCK, (j + 1) * COL_BLOCK)
        h_ref[:, cols] = (jnp.dot(xb, w_ref[:, cols], preferred_element_type=F32)
                          + b_ref[:, cols])


def _inproj(x, w, b):
    n = x.shape[0]
    return pl.pallas_call(
        _inproj_body,
        grid=(n // ROW_TILE,),
        in_specs=[pl.BlockSpec((ROW_TILE, D_MODEL), lambda i: (i, 0)),
                  _const_spec((D_MODEL, IN_CH)),
                  _const_spec((1, IN_CH))],
        out_specs=pl.BlockSpec((ROW_TILE, IN_CH), lambda i: (i, 0)),
        out_shape=jax.ShapeDtypeStruct((n, IN_CH), F32),
        compiler_params=_params("parallel"),
        name="inproj",
    )(x, w, b)


def _conv_gmlp_body(tiles_per_seq, a_ref, halo_ref, c_ref, dww_ref, dwb_ref, cg_ref, cb_ref,
                    gg_ref, gb_ref, ws_ref, bs_ref, o_ref, buf_ref):
    first = (pl.program_id(0) % tiles_per_seq) == 0

    halo = halo_ref[...]
    glu_halo = jnp.where(first, 0.0, halo[:, :CONV_CH] * _sigmoid(halo[:, CONV_CH:]))
    a = a_ref[...]
    glu = a[:, :CONV_CH] * _sigmoid(a[:, CONV_CH:])
    slabs = CONV_CH // LANES
    for s in range(slabs):
        lanes = slice(s * LANES, (s + 1) * LANES)
        buf_ref[s, pl.ds(0, CONV_HALO, stride=2), :] = glu_halo[:, lanes]
        buf_ref[s, pl.ds(2 * CONV_HALO, ROW_TILE, stride=2), :] = glu[:, lanes]

    lead = CONV_HALO - (CONV_WIDTH - 1)

    for r0 in range(0, ROW_TILE, CONV_ROW_CHUNK):
        accs = []
        for s in range(slabs):
            lanes = slice(s * LANES, (s + 1) * LANES)
            acc = jnp.broadcast_to(dwb_ref[:, lanes], (CONV_ROW_CHUNK, LANES))
            for k in range(CONV_WIDTH):
                window = pl.ds(2 * (r0 + lead + k), CONV_ROW_CHUNK, stride=2)
                acc = acc + dww_ref[k:k + 1, lanes] * buf_ref[s, window, :]
            accs.append(acc)
        y = _layernorm(jnp.concatenate(accs, axis=1), cg_ref[...], cb_ref[...])
        o_ref[r0:r0 + CONV_ROW_CHUNK, 0:CONV_CH] = (y * _sigmoid(y)).astype(BF16)

    lane = lax.broadcasted_iota(jnp.int32, (1, GMLP_CH), 1)
    for ch in range(ROW_TILE // CHUNK):
        rows = slice(ch * CHUNK, (ch + 1) * CHUNK)
        c = c_ref[rows, :]
        vn = _layernorm(c[:, GMLP_CH:], gg_ref[...], gb_ref[...])
        stacked = jnp.concatenate(
            [jnp.where((lane // GMLP_GROUP_DIM) == g, vn, 0.0) for g in range(GMLP_GROUPS)],
            axis=0).astype(BF16)
        mixed = jnp.dot(ws_ref[...], stacked, preferred_element_type=F32) + bs_ref[...]
        o_ref[rows, CONV_CH:] = (c[:, :GMLP_CH] * mixed).astype(BF16)


def _conv_gmlp(h, seq, dww, dwb, cg, cb, gg, gb, ws_cat, bs_tab):
    n = h.shape[0]
    tiles_per_seq = seq // ROW_TILE
    halo_per_tile = ROW_TILE // CONV_HALO
    return pl.pallas_call(
        functools.partial(_conv_gmlp_body, tiles_per_seq),
        grid=(n // ROW_TILE,),
        in_specs=[pl.BlockSpec((ROW_TILE, COL_BLOCK), lambda i: (i, COL_A)),
                  pl.BlockSpec((CONV_HALO, COL_BLOCK),
                               lambda i: (jnp.maximum(i * halo_per_tile - 1, 0), COL_A)),
                  pl.BlockSpec((ROW_TILE, COL_BLOCK), lambda i: (i, COL_C)),
                  _const_spec(dww.shape), _const_spec(dwb.shape),
                  _const_spec(cg.shape), _const_spec(cb.shape),
                  _const_spec(gg.shape), _const_spec(gb.shape),
                  _const_spec(ws_cat.shape), _const_spec(bs_tab.shape)],
        out_specs=pl.BlockSpec((ROW_TILE, CONV_CH + GMLP_CH), lambda i: (i, 0)),
        out_shape=jax.ShapeDtypeStruct((n, CONV_CH + GMLP_CH), BF16),
        scratch_shapes=[pltpu.VMEM((CONV_CH // LANES, 2 * (CONV_HALO + ROW_TILE), LANES), F32)],
        compiler_params=_params("parallel"),
        name="conv_gmlp",
    )(h, h, h, dww, dwb, cg, cb, gg, gb, ws_cat, bs_tab)


def _attn_scores(q2, k2, bias_lo, bias_hi):
    lo = lax.broadcasted_iota(jnp.int32, (1, HEAD_PAIR), 1) < HEAD_DIM
    nt = (((1,), (1,)), ((), ()))
    q2 = q2 * (HEAD_DIM ** -0.5 * LOG2_E)
    kb = k2.astype(BF16)
    ps, ms = [], []
    for keep, bias in ((lo, bias_lo), (jnp.logical_not(lo), bias_hi)):
        qh = jnp.where(keep, q2, 0.0).astype(BF16)
        s = lax.dot_general(qh, kb, nt, preferred_element_type=F32) + bias
        m = jnp.max(s, axis=-1, keepdims=True)
        ps.append(jnp.exp2(s - m).astype(BF16))
        ms.append(m)
    return jnp.concatenate(ps, axis=1), jnp.where(lo, ms[0], ms[1])


def _attn_values(p, v2):
    lo = lax.broadcasted_iota(jnp.int32, (1, HEAD_PAIR), 1) < HEAD_DIM
    shape = (2 * ATTN_BLOCK, HEAD_PAIR)
    ind_lo = jnp.broadcast_to(jnp.where(lo, 1.0, 0.0), shape).astype(BF16)
    ind_hi = jnp.broadcast_to(jnp.where(lo, 0.0, 1.0), shape).astype(BF16)
    vb = v2.astype(BF16)
    zero = jnp.zeros_like(vb)
    v_stack = jnp.concatenate(
        [jnp.concatenate([jnp.where(lo, vb, zero), ind_lo], axis=1),
         jnp.concatenate([jnp.where(lo, zero, vb), ind_hi], axis=1)], axis=0)
    ol = jnp.dot(p, v_stack, preferred_element_type=F32)
    return ol[:, :HEAD_PAIR], ol[:, HEAD_PAIR:]


def _attn_body(q_ref, kc_ref, kp_ref, vc_ref, vp_ref, bias_ref, out_ref,
               k_all, v_all, p_scr, o_scr, m_scr, l_scr):
    pair = pl.program_id(1)
    edge = jnp.where(pl.program_id(2) == 0, 0, 1)
    k_all[0:ATTN_TILE, :] = kp_ref[...]
    k_all[ATTN_TILE:, :] = kc_ref[...]
    v_all[0:ATTN_TILE, :] = vp_ref[...]
    v_all[ATTN_TILE:, :] = vc_ref[...]

    def rows(start, n, d):
        return pl.ds(start, n) if d == 1 else pl.ds(start, n, stride=d)

    units = []
    for p, (_, d) in enumerate(DILATED_PATTERNS):
        span = d * ATTN_BLOCK
        for r in range(d):
            for j in range(ATTN_TILE // span):
                units.append((p, d, r + j * span, span, j == 0))
    groups = [units[g:g + ATTN_GROUP] for g in range(0, len(units), ATTN_GROUP)]

    def scores(group, slot):
        for t, (p, d, start, span, seq_edge) in enumerate(group):
            sel = edge if seq_edge else 1
            keys = rows(ATTN_TILE + start - span, 2 * ATTN_BLOCK, d)
            probs, m = _attn_scores(q_ref[rows(start, ATTN_BLOCK, d), :], k_all[keys, :],
                                    bias_ref[p, sel, 2 * pair], bias_ref[p, sel, 2 * pair + 1])
            p_scr[slot, t] = probs
            m_scr[p, rows(start, ATTN_BLOCK, d), :] = m

    def values(group, slot):
        for t, (p, d, start, span, _) in enumerate(group):
            keys = rows(ATTN_TILE + start - span, 2 * ATTN_BLOCK, d)
            o, l = _attn_values(p_scr[slot, t], v_all[keys, :])
            o_scr[p, rows(start, ATTN_BLOCK, d), :] = o
            l_scr[p, rows(start, ATTN_BLOCK, d), :] = l

    scores(groups[0], 0)
    for g in range(1, len(groups)):
        scores(groups[g], g % 2)
        values(groups[g - 1], (g - 1) % 2)
    values(groups[-1], (len(groups) - 1) % 2)

    def merge(i, carry):
        r = pl.ds(pl.multiple_of(i * MERGE_ROWS, MERGE_ROWS), MERGE_ROWS)
        ms = [m_scr[p, r, :] for p in range(len(DILATED_PATTERNS))]
        big = functools.reduce(jnp.maximum, ms)
        ws = [jnp.exp2(m - big) for m in ms]
        num = sum(w * o_scr[p, r, :] for p, w in enumerate(ws))
        den = sum(w * l_scr[p, r, :] for p, w in enumerate(ws))
        out_ref[r, :] = (num / den).astype(BF16)
        return carry

    lax.fori_loop(0, ATTN_TILE // MERGE_ROWS, merge, 0)


def _attention(h, batch, seq, bias):
    n = h.shape[0]
    tiles = seq // ATTN_TILE
    lane_blocks = COL_BLOCK // LANES

    def cur(col):
        return pl.BlockSpec((ATTN_TILE, LANES),
                            lambda b, hp, t: (b * tiles + t, col * lane_blocks + hp))

    def prev(col):
        return pl.BlockSpec((ATTN_TILE, LANES),
                            lambda b, hp, t: (b * tiles + jnp.maximum(t - 1, 0),
                                              col * lane_blocks + hp))

    return pl.pallas_call(
        _attn_body,
        grid=(batch, N_PAIRS, tiles),
        in_specs=[cur(COL_Q), cur(COL_K), prev(COL_K), cur(COL_V), prev(COL_V),
                  _const_spec(bias.shape)],
        out_specs=pl.BlockSpec((ATTN_TILE, LANES), lambda b, hp, t: (b * tiles + t, hp)),
        out_shape=jax.ShapeDtypeStruct((n, ATTN_CH), BF16),
        scratch_shapes=[pltpu.VMEM((2 * ATTN_TILE, LANES), F32),
                        pltpu.VMEM((2 * ATTN_TILE, LANES), F32),
                        pltpu.VMEM((2, ATTN_GROUP, ATTN_BLOCK, 4 * ATTN_BLOCK), BF16),
                        pltpu.VMEM((len(DILATED_PATTERNS), ATTN_TILE, LANES), F32),
                        pltpu.VMEM((len(DILATED_PATTERNS), ATTN_TILE, LANES), F32),
                        pltpu.VMEM((len(DILATED_PATTERNS), ATTN_TILE, LANES), F32)],
        compiler_params=_params("parallel", "parallel", "arbitrary"),
        name="dilated_attn",
    )(h, h, h, h, h, bias)


def _t5_bucket(dist):
    max_exact = N_BUCKETS // 2
    d = np.maximum(dist, 1).astype(np.float64)
    large = max_exact + (np.log(d / max_exact) / math.log(MAX_DISTANCE / max_exact)
                         * (N_BUCKETS - max_exact)).astype(np.int32)
    large = np.minimum(large, N_BUCKETS - 1)
    return np.where(dist < max_exact, dist, large).astype(np.int32)


def _attn_bias(rel_table, window, dilation):
    n_win = window // dilation
    assert n_win <= ATTN_BLOCK
    heads = rel_table.shape[1]
    onehot = np.eye(N_BUCKETS, dtype=np.float32)[_t5_bucket(np.arange(n_win + 1) * dilation)]
    by_dist = jnp.sum(onehot[:, :, None] * rel_table.astype(F32)[None], axis=1)
    period = 3 * ATTN_BLOCK + 1
    u = jnp.concatenate([jnp.full((heads, ATTN_BLOCK - n_win), NEG_LOGIT, F32), by_dist[::-1].T,
                         jnp.full((heads, period - ATTN_BLOCK - 1), NEG_LOGIT, F32)], axis=1)
    toeplitz = jnp.tile(u, (1, ATTN_BLOCK))[:, :ATTN_BLOCK * (period - 1)]
    main = toeplitz.reshape(heads, ATTN_BLOCK, period - 1)[:, :, :2 * ATTN_BLOCK]
    kj = np.arange(2 * ATTN_BLOCK)[None, None, :]
    first = jnp.where(jnp.asarray(kj >= ATTN_BLOCK), main, NEG_LOGIT)
    return jnp.stack([first, main]) * LOG2_E


def _post_body(alpha, tiles_per_seq, ac_ref, attn_ref, x_ref, wo_ref, bo_ref, g1_ref, b1_ref,
               wup_ref, bup_ref, cw_ref, cb_ref, wdn_ref, bdn_ref, g2_ref, b2_ref, y_ref,
               x1_ref, xb_ref, h_buf, act_ref, tail_ref):
    mix_half = CONV_CH + GMLP_CH
    for t in range(POST_TILES):
        rows = slice(t * ROW_TILE, (t + 1) * ROW_TILE)
        mix = (jnp.dot(ac_ref[rows, :], wo_ref[0:mix_half, :], preferred_element_type=F32)
               + jnp.dot(attn_ref[rows, :], wo_ref[mix_half:, :], preferred_element_type=F32)
               + bo_ref[...])
        x1 = _layernorm(alpha * x_ref[rows, :] + mix, g1_ref[...], b1_ref[...])
        x1_ref[t] = x1
        xb_ref[t] = x1.astype(BF16)

    tile0 = pl.program_id(0) * POST_TILES
    firsts = [((tile0 + t) % tiles_per_seq) == 0 for t in range(POST_TILES)]
    half_slabs = FF_CHUNK // LANES
    slabs = 2 * half_slabs
    n_chunks = D_FF // FF_CHUNK

    def chunk_cols(c, half):
        start = half * D_FF + c * FF_CHUNK
        return slice(start, start + FF_CHUNK)

    steps = [(t, c) for t in range(POST_TILES) for c in range(n_chunks)]

    def up(n):
        t, c = steps[n]
        for half in range(2):
            cols = chunk_cols(c, half)
            h = (jnp.dot(xb_ref[t], wup_ref[:, cols], preferred_element_type=F32)
                 + bup_ref[:, cols])
            halo = jnp.where(firsts[t], 0.0, tail_ref[:, cols])
            tail_ref[:, cols] = h[ROW_TILE - FFN_HALO:, :]
            for s in range(half_slabs):
                lanes = slice(s * LANES, (s + 1) * LANES)
                buf = h_buf.at[n % FFN_SLOTS, half * half_slabs + s]
                buf[pl.ds(0, FFN_HALO, stride=2), :] = halo[:, lanes]
                buf[pl.ds(2 * FFN_HALO, ROW_TILE, stride=2), :] = h[:, lanes]

    def conv(n, s, r0):
        start = chunk_cols(steps[n][1], s // half_slabs).start + (s % half_slabs) * LANES
        wcols = slice(start, start + LANES)
        y = cb_ref[:, wcols]
        for k in range(FFN_CONV_WIDTH):
            shift = FFN_CONV_WIDTH - 1 - k
            y = y + cw_ref[k:k + 1, wcols] * h_buf[n % FFN_SLOTS, s, pl.ds(
                2 * (FFN_HALO + r0 - shift), FFN_ROWS, stride=2), :]
        return y

    for n in range(FFN_LOOKAHEAD):
        up(n)
    acc = None
    for n, (t, c) in enumerate(steps):
        if n + FFN_LOOKAHEAD < len(steps):
            up(n + FFN_LOOKAHEAD)
        for r0 in range(0, ROW_TILE, FFN_ROWS):
            for s in range(half_slabs):
                gate, val = conv(n, s, r0), conv(n, s + half_slabs, r0)
                act_ref[n % 2, r0:r0 + FFN_ROWS, s * LANES:(s + 1) * LANES] = (
                    gate * _sigmoid(gate) * val).astype(BF16)
        if c == 0:
            acc = alpha * x1_ref[t] + bdn_ref[...]
        acc = acc + jnp.dot(act_ref[n % 2], wdn_ref[c * FF_CHUNK:(c + 1) * FF_CHUNK, :],
                            preferred_element_type=F32)
        if c == n_chunks - 1:
            y_ref[t * ROW_TILE:(t + 1) * ROW_TILE, :] = _layernorm(acc, g2_ref[...], b2_ref[...])


def _post(alpha, ac, attn, x, seq, wo, bo, g1, b1, wup, bup, cw, cb, wdn, bdn, g2, b2):
    n = x.shape[0]
    rows = POST_TILES * ROW_TILE
    row = lambda width: pl.BlockSpec((rows, width), lambda i: (i, 0))
    consts = [wo, bo, g1, b1, wup, bup, cw, cb, wdn, bdn, g2, b2]
    return pl.pallas_call(
        functools.partial(_post_body, alpha, seq // ROW_TILE),
        grid=(n // rows,),
        in_specs=[row(CONV_CH + GMLP_CH), row(ATTN_CH), row(D_MODEL)]
                 + [_const_spec(c.shape) for c in consts],
        out_specs=row(D_MODEL),
        out_shape=jax.ShapeDtypeStruct((n, D_MODEL), F32),
        scratch_shapes=[pltpu.VMEM((POST_TILES, ROW_TILE, D_MODEL), F32),
                        pltpu.VMEM((POST_TILES, ROW_TILE, D_MODEL), BF16),
                        pltpu.VMEM((FFN_SLOTS, 2 * FF_CHUNK // LANES, 2 * (FFN_HALO + ROW_TILE),
                                    LANES), F32),
                        pltpu.VMEM((2, ROW_TILE, FF_CHUNK), BF16),
                        pltpu.VMEM((FFN_HALO, 2 * D_FF), F32)],
        compiler_params=_params("arbitrary"),
        name="outproj_convffn",
    )(ac, attn, x, *consts)


def kernel(x, w_in, b_in, conv_dw_w, conv_dw_b, conv_ln_g, conv_ln_b, rel_bias_table, gmlp_ln_g, gmlp_ln_b, gmlp_w_s, gmlp_b_s, w_out, b_out, ln1_g, ln1_b, ffn_w_up, ffn_b_up, ffn_conv_w, ffn_conv_b, ffn_w_down, ffn_b_down, ln2_g, ln2_b):
    batch, seq, _ = x.shape
    depth = w_in.shape[0]
    alpha = (2.0 * depth) ** 0.25
    n = batch * seq
    row2d = lambda v: v.reshape(1, -1)
    bias = jnp.stack([_attn_bias(rel_bias_table, w, d) for (w, d) in DILATED_PATTERNS])

    xf = x.reshape(n, D_MODEL)
    for l in range(depth):
        h = _inproj(xf, w_in[l].astype(BF16), row2d(b_in[l]))
        ws_cat = jnp.transpose(jnp.tril(gmlp_w_s[l]), (1, 0, 2)).reshape(CHUNK, GMLP_GROUPS * CHUNK)
        bs_tab = jnp.repeat(gmlp_b_s[l].T, GMLP_GROUP_DIM, axis=1)
        ac = _conv_gmlp(h, seq, conv_dw_w[l], row2d(conv_dw_b[l]), row2d(conv_ln_g[l]),
                        row2d(conv_ln_b[l]), row2d(gmlp_ln_g[l]), row2d(gmlp_ln_b[l]),
                        ws_cat.astype(BF16), bs_tab)
        attn = _attention(h, batch, seq, bias)
        wo = w_out[l].astype(BF16)
        wo = jnp.concatenate([wo[:CONV_CH], wo[CONV_CH + ATTN_CH:], wo[CONV_CH:CONV_CH + ATTN_CH]])
        xf = _post(alpha, ac, attn, xf, seq, wo, row2d(b_out[l]), row2d(ln1_g[l]),
                   row2d(ln1_b[l]), ffn_w_up[l].astype(BF16), row2d(ffn_b_up[l]), ffn_conv_w[l],
                   row2d(ffn_conv_b[l]), ffn_w_down[l].astype(BF16), row2d(ffn_b_down[l]),
                   row2d(ln2_g[l]), row2d(ln2_b[l]))
    return xf.reshape(batch, seq, D_MODEL)
```

```python
import functools
import math

import jax
import jax.numpy as jnp
import numpy as np
from jax import lax
from jax.experimental import pallas as pl
from jax.experimental.pallas import tpu as pltpu

D_MODEL = 1024
HEAD_DIM = 64
CONV_CH = 256
CONV_WIDTH = 31
ATTN_HEADS = 8
ATTN_CH = ATTN_HEADS * HEAD_DIM
DILATED_PATTERNS = ((128, 1), (512, 4), (2048, 16))
ATTN_BLOCK = 128
N_BUCKETS = 32
MAX_DISTANCE = 2048
GMLP_CH = 256
GMLP_GROUPS = 4
GMLP_GROUP_DIM = GMLP_CH // GMLP_GROUPS
CHUNK = 128
MIX_CH = CONV_CH + ATTN_CH + GMLP_CH
IN_CH = 2 * CONV_CH + 3 * ATTN_CH + 2 * GMLP_CH
D_FF = 2816
FFN_CONV_WIDTH = 3
LN_EPS = 1e-5

LANES = 128
COL_BLOCK = 512
COL_A, COL_Q, COL_K, COL_V, COL_C = range(IN_CH // COL_BLOCK)
QKV_Q, QKV_K, QKV_V = range(3)

ROW_TILE = 512
CONV_HALO = 32
CONV_ROW_CHUNK = 64
HEAD_PAIR = 2 * HEAD_DIM
N_PAIRS = ATTN_HEADS // 2
ATTN_TILE = ATTN_BLOCK * max(d for _, d in DILATED_PATTERNS)
ATTN_GROUP = 4
MERGE_ROWS = 256
FF_CHUNK = 256
FFN_LOOKAHEAD = 1
FFN_SLOTS = FFN_LOOKAHEAD + 1
FFN_ROWS = 64
POST_TILES = 1
FFN_HALO = 8
LOG2_E = math.log2(math.e)
NEG_LOGIT = -1e30
VMEM_LIMIT_BYTES = 56 * 1024 * 1024

BF16 = jnp.bfloat16
F32 = jnp.float32


def _const_spec(shape):
    zeros = (0,) * len(shape)
    return pl.BlockSpec(shape, lambda *_: zeros, pipeline_mode=pl.Buffered(1))


def _params(*semantics):
    return pltpu.CompilerParams(dimension_semantics=semantics,
                                vmem_limit_bytes=VMEM_LIMIT_BYTES)


def _layernorm(x, g, b):
    mu = jnp.mean(x, axis=-1, keepdims=True)
    xc = x - mu
    var = jnp.mean(xc * xc, axis=-1, keepdims=True)
    return xc * lax.rsqrt(var + LN_EPS) * g + b


def _sigmoid(x):
    return 1.0 / (1.0 + jnp.exp(-x))


def _inproj_body(tiles_per_seq, x_ref, w_ref, b_ref, dww_ref, dwb_ref, cg_ref, cb_ref,
                 gg_ref, gb_ref, ws_ref, bs_ref, qkv_ref, o_ref, xb_ref, buf_ref, c_ref):
    first = (pl.program_id(0) % tiles_per_seq) == 0
    xb_ref[...] = x_ref[...].astype(BF16)

    @pl.when(pl.program_id(0) == 0)
    def _():
        buf_ref[...] = jnp.zeros_like(buf_ref)

    def project(col):
        cols = slice(col * COL_BLOCK, (col + 1) * COL_BLOCK)
        return jnp.dot(xb_ref[...], w_ref[:, cols], preferred_element_type=F32) + b_ref[:, cols]

    a = project(COL_A)
    glu = a[:, :CONV_CH] * _sigmoid(a[:, CONV_CH:])
    slabs = CONV_CH // LANES
    for s in range(slabs):
        lanes = slice(s * LANES, (s + 1) * LANES)
        prev_tail = buf_ref[s, pl.ds(2 * ROW_TILE, CONV_HALO, stride=2), :]
        buf_ref[s, pl.ds(0, CONV_HALO, stride=2), :] = jnp.where(first, 0.0, prev_tail)
        buf_ref[s, pl.ds(2 * CONV_HALO, ROW_TILE, stride=2), :] = glu[:, lanes]
    c_ref[...] = project(COL_C)
    for j, col in enumerate((COL_Q, COL_K, COL_V)):
        qkv_ref[:, j * COL_BLOCK:(j + 1) * COL_BLOCK] = project(col)

    lead = CONV_HALO - (CONV_WIDTH - 1)

    for r0 in range(0, ROW_TILE, CONV_ROW_CHUNK):
        accs = []
        for s in range(slabs):
            lanes = slice(s * LANES, (s + 1) * LANES)
            acc = jnp.broadcast_to(dwb_ref[:, lanes], (CONV_ROW_CHUNK, LANES))
            for k in range(CONV_WIDTH):
                window = pl.ds(2 * (r0 + lead + k), CONV_ROW_CHUNK, stride=2)
                acc = acc + dww_ref[k:k + 1, lanes] * buf_ref[s, window, :]
            accs.append(acc)
        y = _layernorm(jnp.concatenate(accs, axis=1), cg_ref[...], cb_ref[...])
        o_ref[r0:r0 + CONV_ROW_CHUNK, 0:CONV_CH] = (y * _sigmoid(y)).astype(BF16)

    lane = lax.broadcasted_iota(jnp.int32, (1, GMLP_CH), 1)
    for ch in range(ROW_TILE // CHUNK):
        rows = slice(ch * CHUNK, (ch + 1) * CHUNK)
        c = c_ref[rows, :]
        vn = _layernorm(c[:, GMLP_CH:], gg_ref[...], gb_ref[...])
        stacked = jnp.concatenate(
            [jnp.where((lane // GMLP_GROUP_DIM) == g, vn, 0.0) for g in range(GMLP_GROUPS)],
            axis=0).astype(BF16)
        mixed = jnp.dot(ws_ref[...], stacked, preferred_element_type=F32) + bs_ref[...]
        o_ref[rows, CONV_CH:] = (c[:, :GMLP_CH] * mixed).astype(BF16)


def _inproj(x, seq, w, b, dww, dwb, cg, cb, gg, gb, ws_cat, bs_tab):
    n = x.shape[0]
    consts = [w, b, dww, dwb, cg, cb, gg, gb, ws_cat, bs_tab]
    row = lambda width: pl.BlockSpec((ROW_TILE, width), lambda i: (i, 0))
    return pl.pallas_call(
        functools.partial(_inproj_body, seq // ROW_TILE),
        grid=(n // ROW_TILE,),
        in_specs=[row(D_MODEL)] + [_const_spec(c.shape) for c in consts],
        out_specs=[row(3 * ATTN_CH), row(CONV_CH + GMLP_CH)],
        out_shape=[jax.ShapeDtypeStruct((n, 3 * ATTN_CH), F32),
                   jax.ShapeDtypeStruct((n, CONV_CH + GMLP_CH), BF16)],
        scratch_shapes=[pltpu.VMEM((ROW_TILE, D_MODEL), BF16),
                        pltpu.VMEM((CONV_CH // LANES, 2 * (CONV_HALO + ROW_TILE), LANES), F32),
                        pltpu.VMEM((ROW_TILE, 2 * GMLP_CH), F32)],
        compiler_params=_params("arbitrary"),
        name="inproj_conv_gmlp",
    )(x, *consts)


def _attn_scores(q2, k2, bias_lo, bias_hi):
    lo = lax.broadcasted_iota(jnp.int32, (1, HEAD_PAIR), 1) < HEAD_DIM
    nt = (((1,), (1,)), ((), ()))
    q2 = q2 * (HEAD_DIM ** -0.5 * LOG2_E)
    kb = k2.astype(BF16)
    ps, ms = [], []
    for keep, bias in ((lo, bias_lo), (jnp.logical_not(lo), bias_hi)):
        qh = jnp.where(keep, q2, 0.0).astype(BF16)
        s = lax.dot_general(qh, kb, nt, preferred_element_type=F32) + bias
        m = jnp.max(s, axis=-1, keepdims=True)
        ps.append(jnp.exp2(s - m).astype(BF16))
        ms.append(m)
    return jnp.concatenate(ps, axis=1), jnp.where(lo, ms[0], ms[1])


def _attn_values(p, v2):
    lo = lax.broadcasted_iota(jnp.int32, (1, HEAD_PAIR), 1) < HEAD_DIM
    shape = (2 * ATTN_BLOCK, HEAD_PAIR)
    ind_lo = jnp.broadcast_to(jnp.where(lo, 1.0, 0.0), shape).astype(BF16)
    ind_hi = jnp.broadcast_to(jnp.where(lo, 0.0, 1.0), shape).astype(BF16)
    vb = v2.astype(BF16)
    zero = jnp.zeros_like(vb)
    v_stack = jnp.concatenate(
        [jnp.concatenate([jnp.where(lo, vb, zero), ind_lo], axis=1),
         jnp.concatenate([jnp.where(lo, zero, vb), ind_hi], axis=1)], axis=0)
    ol = jnp.dot(p, v_stack, preferred_element_type=F32)
    return ol[:, :HEAD_PAIR], ol[:, HEAD_PAIR:]


def _attn_body(q_ref, kc_ref, kp_ref, vc_ref, vp_ref, bias_ref, out_ref,
               k_all, v_all, p_scr, o_scr, m_scr, l_scr):
    pair = pl.program_id(1)
    edge = jnp.where(pl.program_id(2) == 0, 0, 1)
    k_all[0:ATTN_TILE, :] = kp_ref[...]
    k_all[ATTN_TILE:, :] = kc_ref[...]
    v_all[0:ATTN_TILE, :] = vp_ref[...]
    v_all[ATTN_TILE:, :] = vc_ref[...]

    def rows(start, n, d):
        return pl.ds(start, n) if d == 1 else pl.ds(start, n, stride=d)

    units = []
    for p, (_, d) in enumerate(DILATED_PATTERNS):
        span = d * ATTN_BLOCK
        for r in range(d):
            for j in range(ATTN_TILE // span):
                units.append((p, d, r + j * span, span, j == 0))
    groups = [units[g:g + ATTN_GROUP] for g in range(0, len(units), ATTN_GROUP)]

    def scores(group, slot):
        for t, (p, d, start, span, seq_edge) in enumerate(group):
            sel = edge if seq_edge else 1
            keys = rows(ATTN_TILE + start - span, 2 * ATTN_BLOCK, d)
            probs, m = _attn_scores(q_ref[rows(start, ATTN_BLOCK, d), :], k_all[keys, :],
                                    bias_ref[p, sel, 2 * pair], bias_ref[p, sel, 2 * pair + 1])
            p_scr[slot, t] = probs
            m_scr[p, rows(start, ATTN_BLOCK, d), :] = m

    def values(group, slot):
        for t, (p, d, start, span, _) in enumerate(group):
            keys = rows(ATTN_TILE + start - span, 2 * ATTN_BLOCK, d)
            o, l = _attn_values(p_scr[slot, t], v_all[keys, :])
            o_scr[p, rows(start, ATTN_BLOCK, d), :] = o
            l_scr[p, rows(start, ATTN_BLOCK, d), :] = l

    scores(groups[0], 0)
    for g in range(1, len(groups)):
        scores(groups[g], g % 2)
        values(groups[g - 1], (g - 1) % 2)
    values(groups[-1], (len(groups) - 1) % 2)

    def merge(i, carry):
        r = pl.ds(pl.multiple_of(i * MERGE_ROWS, MERGE_ROWS), MERGE_ROWS)
        ms = [m_scr[p, r, :] for p in range(len(DILATED_PATTERNS))]
        big = functools.reduce(jnp.maximum, ms)
        ws = [jnp.exp2(m - big) for m in ms]
        num = sum(w * o_scr[p, r, :] for p, w in enumerate(ws))
        den = sum(w * l_scr[p, r, :] for p, w in enumerate(ws))
        out_ref[r, :] = (num / den).astype(BF16)
        return carry

    lax.fori_loop(0, ATTN_TILE // MERGE_ROWS, merge, 0)


def _attention(h, batch, seq, bias):
    n = h.shape[0]
    tiles = seq // ATTN_TILE
    lane_blocks = COL_BLOCK // LANES

    def cur(col):
        return pl.BlockSpec((ATTN_TILE, LANES),
                            lambda b, hp, t: (b * tiles + t, col * lane_blocks + hp))

    def prev(col):
        return pl.BlockSpec((ATTN_TILE, LANES),
                            lambda b, hp, t: (b * tiles + jnp.maximum(t - 1, 0),
                                              col * lane_blocks + hp))

    return pl.pallas_call(
        _attn_body,
        grid=(batch, N_PAIRS, tiles),
        in_specs=[cur(QKV_Q), cur(QKV_K), prev(QKV_K), cur(QKV_V), prev(QKV_V),
                  _const_spec(bias.shape)],
        out_specs=pl.BlockSpec((ATTN_TILE, LANES), lambda b, hp, t: (b * tiles + t, hp)),
        out_shape=jax.ShapeDtypeStruct((n, ATTN_CH), BF16),
        scratch_shapes=[pltpu.VMEM((2 * ATTN_TILE, LANES), F32),
                        pltpu.VMEM((2 * ATTN_TILE, LANES), F32),
                        pltpu.VMEM((2, ATTN_GROUP, ATTN_BLOCK, 4 * ATTN_BLOCK), BF16),
                        pltpu.VMEM((len(DILATED_PATTERNS), ATTN_TILE, LANES), F32),
                        pltpu.VMEM((len(DILATED_PATTERNS), ATTN_TILE, LANES), F32),
                        pltpu.VMEM((len(DILATED_PATTERNS), ATTN_TILE, LANES), F32)],
        compiler_params=_params("parallel", "parallel", "arbitrary"),
        name="dilated_attn",
    )(h, h, h, h, h, bias)


def _t5_bucket(dist):
    max_exact = N_BUCKETS // 2
    d = np.maximum(dist, 1).astype(np.float64)
    large = max_exact + (np.log(d / max_exact) / math.log(MAX_DISTANCE / max_exact)
                         * (N_BUCKETS - max_exact)).astype(np.int32)
    large = np.minimum(large, N_BUCKETS - 1)
    return np.where(dist < max_exact, dist, large).astype(np.int32)


def _attn_bias(rel_table, window, dilation):
    n_win = window // dilation
    assert n_win <= ATTN_BLOCK
    heads = rel_table.shape[1]
    onehot = np.eye(N_BUCKETS, dtype=np.float32)[_t5_bucket(np.arange(n_win + 1) * dilation)]
    by_dist = jnp.sum(onehot[:, :, None] * rel_table.astype(F32)[None], axis=1)
    period = 3 * ATTN_BLOCK + 1
    u = jnp.concatenate([jnp.full((heads, ATTN_BLOCK - n_win), NEG_LOGIT, F32), by_dist[::-1].T,
                         jnp.full((heads, period - ATTN_BLOCK - 1), NEG_LOGIT, F32)], axis=1)
    toeplitz = jnp.tile(u, (1, ATTN_BLOCK))[:, :ATTN_BLOCK * (period - 1)]
    main = toeplitz.reshape(heads, ATTN_BLOCK, period - 1)[:, :, :2 * ATTN_BLOCK]
    kj = np.arange(2 * ATTN_BLOCK)[None, None, :]
    first = jnp.where(jnp.asarray(kj >= ATTN_BLOCK), main, NEG_LOGIT)
    return jnp.stack([first, main]) * LOG2_E


def _post_body(alpha, tiles_per_seq, ac_ref, attn_ref, x_ref, wo_ref, bo_ref, g1_ref, b1_ref,
               wup_ref, bup_ref, cw_ref, cb_ref, wdn_ref, bdn_ref, g2_ref, b2_ref, y_ref,
               x1_ref, xb_ref, h_buf, act_ref, tail_ref):
    mix_half = CONV_CH + GMLP_CH
    for t in range(POST_TILES):
        rows = slice(t * ROW_TILE, (t + 1) * ROW_TILE)
        mix = (jnp.dot(ac_ref[rows, :], wo_ref[0:mix_half, :], preferred_element_type=F32)
               + jnp.dot(attn_ref[rows, :], wo_ref[mix_half:, :], preferred_element_type=F32)
               + bo_ref[...])
        x1 = _layernorm(alpha * x_ref[rows, :] + mix, g1_ref[...], b1_ref[...])
        x1_ref[t] = x1
        xb_ref[t] = x1.astype(BF16)

    tile0 = pl.program_id(0) * POST_TILES
    firsts = [((tile0 + t) % tiles_per_seq) == 0 for t in range(POST_TILES)]
    half_slabs = FF_CHUNK // LANES
    slabs = 2 * half_slabs
    n_chunks = D_FF // FF_CHUNK

    def chunk_cols(c, half):
        start = half * D_FF + c * FF_CHUNK
        return slice(start, start + FF_CHUNK)

    steps = [(t, c) for t in range(POST_TILES) for c in range(n_chunks)]

    def up(n):
        t, c = steps[n]
        for half in range(2):
            cols = chunk_cols(c, half)
            h = (jnp.dot(xb_ref[t], wup_ref[:, cols], preferred_element_type=F32)
                 + bup_ref[:, cols])
            halo = jnp.where(firsts[t], 0.0, tail_ref[:, cols])
            tail_ref[:, cols] = h[ROW_TILE - FFN_HALO:, :]
            for s in range(half_slabs):
                lanes = slice(s * LANES, (s + 1) * LANES)
                buf = h_buf.at[n % FFN_SLOTS, half * half_slabs + s]
                buf[pl.ds(0, FFN_HALO, stride=2), :] = halo[:, lanes]
                buf[pl.ds(2 * FFN_HALO, ROW_TILE, stride=2), :] = h[:, lanes]

    def conv(n, s, r0):
        start = chunk_cols(steps[n][1], s // half_slabs).start + (s % half_slabs) * LANES
        wcols = slice(start, start + LANES)
        y = cb_ref[:, wcols]
        for k in range(FFN_CONV_WIDTH):
            shift = FFN_CONV_WIDTH - 1 - k
            y = y + cw_ref[k:k + 1, wcols] * h_buf[n % FFN_SLOTS, s, pl.ds(
                2 * (FFN_HALO + r0 - shift), FFN_ROWS, stride=2), :]
        return y

    for n in range(FFN_LOOKAHEAD):
        up(n)
    acc = None
    for n, (t, c) in enumerate(steps):
        if n + FFN_LOOKAHEAD < len(steps):
            up(n + FFN_LOOKAHEAD)
        for r0 in range(0, ROW_TILE, FFN_ROWS):
            for s in range(half_slabs):
                gate, val = conv(n, s, r0), conv(n, s + half_slabs, r0)
                act_ref[n % 2, r0:r0 + FFN_ROWS, s * LANES:(s + 1) * LANES] = (
                    gate * _sigmoid(gate) * val).astype(BF16)
        if c == 0:
            acc = alpha * x1_ref[t] + bdn_ref[...]
        acc = acc + jnp.dot(act_ref[n % 2], wdn_ref[c * FF_CHUNK:(c + 1) * FF_CHUNK, :],
                            preferred_element_type=F32)
        if c == n_chunks - 1:
            y_ref[t * ROW_TILE:(t + 1) * ROW_TILE, :] = _layernorm(acc, g2_ref[...], b2_ref[...])


def _post(alpha, ac, attn, x, seq, wo, bo, g1, b1, wup, bup, cw, cb, wdn, bdn, g2, b2):
    n = x.shape[0]
    rows = POST_TILES * ROW_TILE
    row = lambda width: pl.BlockSpec((rows, width), lambda i: (i, 0))
    consts = [wo, bo, g1, b1, wup, bup, cw, cb, wdn, bdn, g2, b2]
    return pl.pallas_call(
        functools.partial(_post_body, alpha, seq // ROW_TILE),
        grid=(n // rows,),
        in_specs=[row(CONV_CH + GMLP_CH), row(ATTN_CH), row(D_MODEL)]
                 + [_const_spec(c.shape) for c in consts],
        out_specs=row(D_MODEL),
        out_shape=jax.ShapeDtypeStruct((n, D_MODEL), F32),
        scratch_shapes=[pltpu.VMEM((POST_TILES, ROW_TILE, D_MODEL), F32),
                        pltpu.VMEM((POST_TILES, ROW_TILE, D_MODEL), BF16),
                        pltpu.VMEM((FFN_SLOTS, 2 * FF_CHUNK // LANES, 2 * (FFN_HALO + ROW_TILE),
                                    LANES), F32),
                        pltpu.VMEM((2, ROW_TILE, FF_CHUNK), BF16),
                        pltpu.VMEM((FFN_HALO, 2 * D_FF), F32)],
        compiler_params=_params("arbitrary"),
        name="outproj_convffn",
    )(ac, attn, x, *consts)


def kernel(x, w_in, b_in, conv_dw_w, conv_dw_b, conv_ln_g, conv_ln_b, rel_bias_table, gmlp_ln_g, gmlp_ln_b, gmlp_w_s, gmlp_b_s, w_out, b_out, ln1_g, ln1_b, ffn_w_up, ffn_b_up, ffn_conv_w, ffn_conv_b, ffn_w_down, ffn_b_down, ln2_g, ln2_b):
    batch, seq, _ = x.shape
    depth = w_in.shape[0]
    alpha = (2.0 * depth) ** 0.25
    n = batch * seq
    row2d = lambda v: v.reshape(1, -1)
    bias = jnp.stack([_attn_bias(rel_bias_table, w, d) for (w, d) in DILATED_PATTERNS])

    xf = x.reshape(n, D_MODEL)
    for l in range(depth):
        ws_cat = jnp.transpose(jnp.tril(gmlp_w_s[l]), (1, 0, 2)).reshape(CHUNK, GMLP_GROUPS * CHUNK)
        bs_tab = jnp.repeat(gmlp_b_s[l].T, GMLP_GROUP_DIM, axis=1)
        qkv, ac = _inproj(xf, seq, w_in[l].astype(BF16), row2d(b_in[l]), conv_dw_w[l],
                          row2d(conv_dw_b[l]), row2d(conv_ln_g[l]), row2d(conv_ln_b[l]),
                          row2d(gmlp_ln_g[l]), row2d(gmlp_ln_b[l]), ws_cat.astype(BF16), bs_tab)
        attn = _attention(qkv, batch, seq, bias)
        wo = w_out[l].astype(BF16)
        wo = jnp.concatenate([wo[:CONV_CH], wo[CONV_CH + ATTN_CH:], wo[CONV_CH:CONV_CH + ATTN_CH]])
        xf = _post(alpha, ac, attn, xf, seq, wo, row2d(b_out[l]), row2d(ln1_g[l]),
                   row2d(ln1_b[l]), ffn_w_up[l].astype(BF16), row2d(ffn_b_up[l]), ffn_conv_w[l],
                   row2d(ffn_conv_b[l]), ffn_w_down[l].astype(BF16), row2d(ffn_b_down[l]),
                   row2d(ln2_g[l]), row2d(ln2_b[l]))
    return xf.reshape(batch, seq, D_MODEL)
```

```python
import functools
import math

import jax
import jax.numpy as jnp
import numpy as np
from jax import lax
from jax.experimental import pallas as pl
from jax.experimental.pallas import tpu as pltpu

D_MODEL = 1024
HEAD_DIM = 64
CONV_CH = 256
CONV_WIDTH = 31
ATTN_HEADS = 8
ATTN_CH = ATTN_HEADS * HEAD_DIM
DILATED_PATTERNS = ((128, 1), (512, 4), (2048, 16))
ATTN_BLOCK = 128
N_BUCKETS = 32
MAX_DISTANCE = 2048
GMLP_CH = 256
GMLP_GROUPS = 4
GMLP_GROUP_DIM = GMLP_CH // GMLP_GROUPS
CHUNK = 128
MIX_CH = CONV_CH + ATTN_CH + GMLP_CH
IN_CH = 2 * CONV_CH + 3 * ATTN_CH + 2 * GMLP_CH
D_FF = 2816
FFN_CONV_WIDTH = 3
LN_EPS = 1e-5

LANES = 128
COL_BLOCK = 512
COL_A, COL_Q, COL_K, COL_V, COL_C = range(IN_CH // COL_BLOCK)
QKV_Q, QKV_K, QKV_V = range(3)

ROW_TILE = 512
CONV_HALO = 32
CONV_ROW_CHUNK = 64
HEAD_PAIR = 2 * HEAD_DIM
N_PAIRS = ATTN_HEADS // 2
ATTN_TILE = ATTN_BLOCK * max(d for _, d in DILATED_PATTERNS)
ATTN_AHEAD = 1
ATTN_GROUP = 1
FF_CHUNK = 256
FFN_LOOKAHEAD = 1
FFN_SLOTS = FFN_LOOKAHEAD + 1
FFN_ROWS = 64
POST_TILES = 1
FFN_HALO = 8
LOG2_E = math.log2(math.e)
NEG_LOGIT = -1e30
VMEM_LIMIT_BYTES = 56 * 1024 * 1024

BF16 = jnp.bfloat16
F32 = jnp.float32


def _const_spec(shape):
    zeros = (0,) * len(shape)
    return pl.BlockSpec(shape, lambda *_: zeros, pipeline_mode=pl.Buffered(1))


def _params(*semantics):
    return pltpu.CompilerParams(dimension_semantics=semantics,
                                vmem_limit_bytes=VMEM_LIMIT_BYTES)


def _layernorm(x, g, b):
    mu = jnp.mean(x, axis=-1, keepdims=True)
    xc = x - mu
    var = jnp.mean(xc * xc, axis=-1, keepdims=True)
    return xc * lax.rsqrt(var + LN_EPS) * g + b


def _sigmoid(x):
    return 1.0 / (1.0 + jnp.exp(-x))


def _inproj_body(tiles_per_seq, x_ref, w_ref, b_ref, dww_ref, dwb_ref, cg_ref, cb_ref,
                 gg_ref, gb_ref, ws_ref, bs_ref, qkv_ref, o_ref, xb_ref, buf_ref, c_ref):
    first = (pl.program_id(0) % tiles_per_seq) == 0
    xb_ref[...] = x_ref[...].astype(BF16)

    @pl.when(pl.program_id(0) == 0)
    def _():
        buf_ref[...] = jnp.zeros_like(buf_ref)

    def project(col):
        cols = slice(col * COL_BLOCK, (col + 1) * COL_BLOCK)
        return jnp.dot(xb_ref[...], w_ref[:, cols], preferred_element_type=F32) + b_ref[:, cols]

    a = project(COL_A)
    glu = a[:, :CONV_CH] * _sigmoid(a[:, CONV_CH:])
    slabs = CONV_CH // LANES
    for s in range(slabs):
        lanes = slice(s * LANES, (s + 1) * LANES)
        prev_tail = buf_ref[s, pl.ds(2 * ROW_TILE, CONV_HALO, stride=2), :]
        buf_ref[s, pl.ds(0, CONV_HALO, stride=2), :] = jnp.where(first, 0.0, prev_tail)
        buf_ref[s, pl.ds(2 * CONV_HALO, ROW_TILE, stride=2), :] = glu[:, lanes]
    c_ref[...] = project(COL_C)
    for j, col in enumerate((COL_Q, COL_K, COL_V)):
        qkv_ref[:, j * COL_BLOCK:(j + 1) * COL_BLOCK] = project(col)

    lead = CONV_HALO - (CONV_WIDTH - 1)

    for r0 in range(0, ROW_TILE, CONV_ROW_CHUNK):
        accs = []
        for s in range(slabs):
            lanes = slice(s * LANES, (s + 1) * LANES)
            acc = jnp.broadcast_to(dwb_ref[:, lanes], (CONV_ROW_CHUNK, LANES))
            for k in range(CONV_WIDTH):
                window = pl.ds(2 * (r0 + lead + k), CONV_ROW_CHUNK, stride=2)
                acc = acc + dww_ref[k:k + 1, lanes] * buf_ref[s, window, :]
            accs.append(acc)
        y = _layernorm(jnp.concatenate(accs, axis=1), cg_ref[...], cb_ref[...])
        o_ref[r0:r0 + CONV_ROW_CHUNK, 0:CONV_CH] = (y * _sigmoid(y)).astype(BF16)

    lane = lax.broadcasted_iota(jnp.int32, (1, GMLP_CH), 1)
    for ch in range(ROW_TILE // CHUNK):
        rows = slice(ch * CHUNK, (ch + 1) * CHUNK)
        c = c_ref[rows, :]
        vn = _layernorm(c[:, GMLP_CH:], gg_ref[...], gb_ref[...])
        stacked = jnp.concatenate(
            [jnp.where((lane // GMLP_GROUP_DIM) == g, vn, 0.0) for g in range(GMLP_GROUPS)],
            axis=0).astype(BF16)
        mixed = jnp.dot(ws_ref[...], stacked, preferred_element_type=F32) + bs_ref[...]
        o_ref[rows, CONV_CH:] = (c[:, :GMLP_CH] * mixed).astype(BF16)


def _inproj(x, seq, w, b, dww, dwb, cg, cb, gg, gb, ws_cat, bs_tab):
    n = x.shape[0]
    consts = [w, b, dww, dwb, cg, cb, gg, gb, ws_cat, bs_tab]
    row = lambda width: pl.BlockSpec((ROW_TILE, width), lambda i: (i, 0))
    return pl.pallas_call(
        functools.partial(_inproj_body, seq // ROW_TILE),
        grid=(n // ROW_TILE,),
        in_specs=[row(D_MODEL)] + [_const_spec(c.shape) for c in consts],
        out_specs=[row(3 * ATTN_CH), row(CONV_CH + GMLP_CH)],
        out_shape=[jax.ShapeDtypeStruct((n, 3 * ATTN_CH), F32),
                   jax.ShapeDtypeStruct((n, CONV_CH + GMLP_CH), BF16)],
        scratch_shapes=[pltpu.VMEM((ROW_TILE, D_MODEL), BF16),
                        pltpu.VMEM((CONV_CH // LANES, 2 * (CONV_HALO + ROW_TILE), LANES), F32),
                        pltpu.VMEM((ROW_TILE, 2 * GMLP_CH), F32)],
        compiler_params=_params("arbitrary"),
        name="inproj_conv_gmlp",
    )(x, *consts)


def _attn_scores(q2, k2, bias_lo, bias_hi):
    lo = lax.broadcasted_iota(jnp.int32, (1, HEAD_PAIR), 1) < HEAD_DIM
    nt = (((1,), (1,)), ((), ()))
    q2 = q2 * (HEAD_DIM ** -0.5 * LOG2_E)
    kb = k2.astype(BF16)
    ps, ms = [], []
    for keep, bias in ((lo, bias_lo), (jnp.logical_not(lo), bias_hi)):
        qh = jnp.where(keep, q2, 0.0).astype(BF16)
        s = lax.dot_general(qh, kb, nt, preferred_element_type=F32) + bias
        m = jnp.max(s, axis=-1, keepdims=True)
        ps.append(jnp.exp2(s - m).astype(BF16))
        ms.append(m)
    return jnp.concatenate(ps, axis=1), jnp.where(lo, ms[0], ms[1])


def _attn_values(p, v2):
    lo = lax.broadcasted_iota(jnp.int32, (1, HEAD_PAIR), 1) < HEAD_DIM
    shape = (2 * ATTN_BLOCK, HEAD_PAIR)
    ind_lo = jnp.broadcast_to(jnp.where(lo, 1.0, 0.0), shape).astype(BF16)
    ind_hi = jnp.broadcast_to(jnp.where(lo, 0.0, 1.0), shape).astype(BF16)
    vb = v2.astype(BF16)
    zero = jnp.zeros_like(vb)
    v_stack = jnp.concatenate(
        [jnp.concatenate([jnp.where(lo, vb, zero), ind_lo], axis=1),
         jnp.concatenate([jnp.where(lo, zero, vb), ind_hi], axis=1)], axis=0)
    ol = jnp.dot(p, v_stack, preferred_element_type=F32)
    return ol[:, :HEAD_PAIR], ol[:, HEAD_PAIR:]


def _attn_body(q_ref, kc_ref, kp_ref, vc_ref, vp_ref, bias_ref, out_ref,
               p_scr, o_scr, m_scr, l_scr):
    pair = pl.program_id(1)
    edge = jnp.where(pl.program_id(2) == 0, 0, 1)

    def rows(start, n, d):
        return pl.ds(start, n) if d == 1 else pl.ds(start, n, stride=d)

    def window(cur_ref, prev_ref, d, start, span, seq_edge):
        if seq_edge:
            return jnp.concatenate(
                [prev_ref[rows(ATTN_TILE + start - span, ATTN_BLOCK, d), :],
                 cur_ref[rows(start, ATTN_BLOCK, d), :]], axis=0)
        return cur_ref[rows(start - span, 2 * ATTN_BLOCK, d), :]

    order = sorted(range(len(DILATED_PATTERNS)), key=lambda p: -DILATED_PATTERNS[p][1])
    assert DILATED_PATTERNS[order[-1]][1] == 1
    units = []
    for p in order:
        d = DILATED_PATTERNS[p][1]
        span = d * ATTN_BLOCK
        for r in range(d):
            for j in range(ATTN_TILE // span):
                units.append((p, d, r + j * span, span, j == 0))
    groups = [units[g:g + ATTN_GROUP] for g in range(0, len(units), ATTN_GROUP)]

    def merge(start):
        r = slice(start, start + ATTN_BLOCK)
        ms = [m_scr[p, r, :] for p in range(len(DILATED_PATTERNS))]
        big = functools.reduce(jnp.maximum, ms)
        ws = [jnp.exp2(m - big) for m in ms]
        num = sum(w * o_scr[p, r, :] for p, w in enumerate(ws))
        den = sum(w * l_scr[p, r, :] for p, w in enumerate(ws))
        out_ref[r, :] = (num / den).astype(BF16)

    def scores(group, slot):
        for t, (p, d, start, span, seq_edge) in enumerate(group):
            sel = edge if seq_edge else 1
            probs, m = _attn_scores(q_ref[rows(start, ATTN_BLOCK, d), :],
                                    window(kc_ref, kp_ref, d, start, span, seq_edge),
                                    bias_ref[p, sel, 2 * pair], bias_ref[p, sel, 2 * pair + 1])
            p_scr[slot, t] = probs
            m_scr[p, rows(start, ATTN_BLOCK, d), :] = m

    def values(group, slot):
        for t, (p, d, start, span, seq_edge) in enumerate(group):
            o, l = _attn_values(p_scr[slot, t], window(vc_ref, vp_ref, d, start, span, seq_edge))
            o_scr[p, rows(start, ATTN_BLOCK, d), :] = o
            l_scr[p, rows(start, ATTN_BLOCK, d), :] = l
            if d == 1:
                merge(start)

    slots = ATTN_AHEAD + 1
    for g in range(ATTN_AHEAD):
        scores(groups[g], g % slots)
    for g in range(len(groups)):
        if g + ATTN_AHEAD < len(groups):
            scores(groups[g + ATTN_AHEAD], (g + ATTN_AHEAD) % slots)
        values(groups[g], g % slots)


def _attention(h, batch, seq, bias):
    n = h.shape[0]
    tiles = seq // ATTN_TILE
    lane_blocks = COL_BLOCK // LANES

    def cur(col):
        return pl.BlockSpec((ATTN_TILE, LANES),
                            lambda b, hp, t: (b * tiles + t, col * lane_blocks + hp))

    def prev(col):
        return pl.BlockSpec((ATTN_TILE, LANES),
                            lambda b, hp, t: (b * tiles + jnp.maximum(t - 1, 0),
                                              col * lane_blocks + hp))

    return pl.pallas_call(
        _attn_body,
        grid=(batch, N_PAIRS, tiles),
        in_specs=[cur(QKV_Q), cur(QKV_K), prev(QKV_K), cur(QKV_V), prev(QKV_V),
                  _const_spec(bias.shape)],
        out_specs=pl.BlockSpec((ATTN_TILE, LANES), lambda b, hp, t: (b * tiles + t, hp)),
        out_shape=jax.ShapeDtypeStruct((n, ATTN_CH), BF16),
        scratch_shapes=[pltpu.VMEM((ATTN_AHEAD + 1, ATTN_GROUP, ATTN_BLOCK, 4 * ATTN_BLOCK), BF16),
                        pltpu.VMEM((len(DILATED_PATTERNS), ATTN_TILE, LANES), F32),
                        pltpu.VMEM((len(DILATED_PATTERNS), ATTN_TILE, LANES), F32),
                        pltpu.VMEM((len(DILATED_PATTERNS), ATTN_TILE, LANES), F32)],
        compiler_params=_params("parallel", "parallel", "arbitrary"),
        name="dilated_attn",
    )(h, h, h, h, h, bias)


def _t5_bucket(dist):
    max_exact = N_BUCKETS // 2
    d = np.maximum(dist, 1).astype(np.float64)
    large = max_exact + (np.log(d / max_exact) / math.log(MAX_DISTANCE / max_exact)
                         * (N_BUCKETS - max_exact)).astype(np.int32)
    large = np.minimum(large, N_BUCKETS - 1)
    return np.where(dist < max_exact, dist, large).astype(np.int32)


def _attn_bias(rel_table, window, dilation):
    n_win = window // dilation
    assert n_win <= ATTN_BLOCK
    heads = rel_table.shape[1]
    onehot = np.eye(N_BUCKETS, dtype=np.float32)[_t5_bucket(np.arange(n_win + 1) * dilation)]
    by_dist = jnp.sum(onehot[:, :, None] * rel_table.astype(F32)[None], axis=1)
    period = 3 * ATTN_BLOCK + 1
    u = jnp.concatenate([jnp.full((heads, ATTN_BLOCK - n_win), NEG_LOGIT, F32), by_dist[::-1].T,
                         jnp.full((heads, period - ATTN_BLOCK - 1), NEG_LOGIT, F32)], axis=1)
    toeplitz = jnp.tile(u, (1, ATTN_BLOCK))[:, :ATTN_BLOCK * (period - 1)]
    main = toeplitz.reshape(heads, ATTN_BLOCK, period - 1)[:, :, :2 * ATTN_BLOCK]
    kj = np.arange(2 * ATTN_BLOCK)[None, None, :]
    first = jnp.where(jnp.asarray(kj >= ATTN_BLOCK), main, NEG_LOGIT)
    return jnp.stack([first, main]) * LOG2_E


def _post_body(alpha, tiles_per_seq, ac_ref, attn_ref, x_ref, wo_ref, bo_ref, g1_ref, b1_ref,
               wup_ref, bup_ref, cw_ref, cb_ref, wdn_ref, bdn_ref, g2_ref, b2_ref, y_ref,
               x1_ref, xb_ref, h_buf, act_ref, tail_ref):
    mix_half = CONV_CH + GMLP_CH
    for t in range(POST_TILES):
        rows = slice(t * ROW_TILE, (t + 1) * ROW_TILE)
        mix = (jnp.dot(ac_ref[rows, :], wo_ref[0:mix_half, :], preferred_element_type=F32)
               + jnp.dot(attn_ref[rows, :], wo_ref[mix_half:, :], preferred_element_type=F32)
               + bo_ref[...])
        x1 = _layernorm(alpha * x_ref[rows, :] + mix, g1_ref[...], b1_ref[...])
        x1_ref[t] = x1
        xb_ref[t] = x1.astype(BF16)

    tile0 = pl.program_id(0) * POST_TILES
    firsts = [((tile0 + t) % tiles_per_seq) == 0 for t in range(POST_TILES)]
    half_slabs = FF_CHUNK // LANES
    slabs = 2 * half_slabs
    n_chunks = D_FF // FF_CHUNK

    def chunk_cols(c, half):
        start = half * D_FF + c * FF_CHUNK
        return slice(start, start + FF_CHUNK)

    steps = [(t, c) for t in range(POST_TILES) for c in range(n_chunks)]

    def up(n):
        t, c = steps[n]
        for half in range(2):
            cols = chunk_cols(c, half)
            h = (jnp.dot(xb_ref[t], wup_ref[:, cols], preferred_element_type=F32)
                 + bup_ref[:, cols])
            halo = jnp.where(firsts[t], 0.0, tail_ref[:, cols])
            tail_ref[:, cols] = h[ROW_TILE - FFN_HALO:, :]
            for s in range(half_slabs):
                lanes = slice(s * LANES, (s + 1) * LANES)
                buf = h_buf.at[n % FFN_SLOTS, half * half_slabs + s]
                buf[pl.ds(0, FFN_HALO, stride=2), :] = halo[:, lanes]
                buf[pl.ds(2 * FFN_HALO, ROW_TILE, stride=2), :] = h[:, lanes]

    def conv(n, s, r0):
        start = chunk_cols(steps[n][1], s // half_slabs).start + (s % half_slabs) * LANES
        wcols = slice(start, start + LANES)
        y = cb_ref[:, wcols]
        for k in range(FFN_CONV_WIDTH):
            shift = FFN_CONV_WIDTH - 1 - k
            y = y + cw_ref[k:k + 1, wcols] * h_buf[n % FFN_SLOTS, s, pl.ds(
                2 * (FFN_HALO + r0 - shift), FFN_ROWS, stride=2), :]
        return y

    for n in range(FFN_LOOKAHEAD):
        up(n)
    acc = None
    for n, (t, c) in enumerate(steps):
        if n + FFN_LOOKAHEAD < len(steps):
            up(n + FFN_LOOKAHEAD)
        for r0 in range(0, ROW_TILE, FFN_ROWS):
            for s in range(half_slabs):
                gate, val = conv(n, s, r0), conv(n, s + half_slabs, r0)
                act_ref[n % 2, r0:r0 + FFN_ROWS, s * LANES:(s + 1) * LANES] = (
                    gate * _sigmoid(gate) * val).astype(BF16)
        if c == 0:
            acc = alpha * x1_ref[t] + bdn_ref[...]
        acc = acc + jnp.dot(act_ref[n % 2], wdn_ref[c * FF_CHUNK:(c + 1) * FF_CHUNK, :],
                            preferred_element_type=F32)
        if c == n_chunks - 1:
            y_ref[t * ROW_TILE:(t + 1) * ROW_TILE, :] = _layernorm(acc, g2_ref[...], b2_ref[...])


def _post(alpha, ac, attn, x, seq, wo, bo, g1, b1, wup, bup, cw, cb, wdn, bdn, g2, b2):
    n = x.shape[0]
    rows = POST_TILES * ROW_TILE
    row = lambda width: pl.BlockSpec((rows, width), lambda i: (i, 0))
    consts = [wo, bo, g1, b1, wup, bup, cw, cb, wdn, bdn, g2, b2]
    return pl.pallas_call(
        functools.partial(_post_body, alpha, seq // ROW_TILE),
        grid=(n // rows,),
        in_specs=[row(CONV_CH + GMLP_CH), row(ATTN_CH), row(D_MODEL)]
                 + [_const_spec(c.shape) for c in consts],
        out_specs=row(D_MODEL),
        out_shape=jax.ShapeDtypeStruct((n, D_MODEL), F32),
        scratch_shapes=[pltpu.VMEM((POST_TILES, ROW_TILE, D_MODEL), F32),
                        pltpu.VMEM((POST_TILES, ROW_TILE, D_MODEL), BF16),
                        pltpu.VMEM((FFN_SLOTS, 2 * FF_CHUNK // LANES, 2 * (FFN_HALO + ROW_TILE),
                                    LANES), F32),
                        pltpu.VMEM((2, ROW_TILE, FF_CHUNK), BF16),
                        pltpu.VMEM((FFN_HALO, 2 * D_FF), F32)],
        compiler_params=_params("arbitrary"),
        name="outproj_convffn",
    )(ac, attn, x, *consts)


def kernel(x, w_in, b_in, conv_dw_w, conv_dw_b, conv_ln_g, conv_ln_b, rel_bias_table, gmlp_ln_g, gmlp_ln_b, gmlp_w_s, gmlp_b_s, w_out, b_out, ln1_g, ln1_b, ffn_w_up, ffn_b_up, ffn_conv_w, ffn_conv_b, ffn_w_down, ffn_b_down, ln2_g, ln2_b):
    batch, seq, _ = x.shape
    depth = w_in.shape[0]
    alpha = (2.0 * depth) ** 0.25
    n = batch * seq
    row2d = lambda v: v.reshape(1, -1)
    bias = jnp.stack([_attn_bias(rel_bias_table, w, d) for (w, d) in DILATED_PATTERNS])

    xf = x.reshape(n, D_MODEL)
    for l in range(depth):
        ws_cat = jnp.transpose(jnp.tril(gmlp_w_s[l]), (1, 0, 2)).reshape(CHUNK, GMLP_GROUPS * CHUNK)
        bs_tab = jnp.repeat(gmlp_b_s[l].T, GMLP_GROUP_DIM, axis=1)
        qkv, ac = _inproj(xf, seq, w_in[l].astype(BF16), row2d(b_in[l]), conv_dw_w[l],
                          row2d(conv_dw_b[l]), row2d(conv_ln_g[l]), row2d(conv_ln_b[l]),
                          row2d(gmlp_ln_g[l]), row2d(gmlp_ln_b[l]), ws_cat.astype(BF16), bs_tab)
        attn = _attention(qkv, batch, seq, bias)
        wo = w_out[l].astype(BF16)
        wo = jnp.concatenate([wo[:CONV_CH], wo[CONV_CH + ATTN_CH:], wo[CONV_CH:CONV_CH + ATTN_CH]])
        xf = _post(alpha, ac, attn, xf, seq, wo, row2d(b_out[l]), row2d(ln1_g[l]),
                   row2d(ln1_b[l]), ffn_w_up[l].astype(BF16), row2d(ffn_b_up[l]), ffn_conv_w[l],
                   row2d(ffn_conv_b[l]), ffn_w_down[l].astype(BF16), row2d(ffn_b_down[l]),
                   row2d(ln2_g[l]), row2d(ln2_b[l]))
    return xf.reshape(batch, seq, D_MODEL)
```

```python
import functools
import math

import jax
import jax.numpy as jnp
import numpy as np
from jax import lax
from jax.experimental import pallas as pl
from jax.experimental.pallas import tpu as pltpu

D_MODEL = 1024
HEAD_DIM = 64
CONV_CH = 256
CONV_WIDTH = 31
ATTN_HEADS = 8
ATTN_CH = ATTN_HEADS * HEAD_DIM
DILATED_PATTERNS = ((128, 1), (512, 4), (2048, 16))
ATTN_BLOCK = 128
N_BUCKETS = 32
MAX_DISTANCE = 2048
GMLP_CH = 256
GMLP_GROUPS = 4
GMLP_GROUP_DIM = GMLP_CH // GMLP_GROUPS
CHUNK = 128
MIX_CH = CONV_CH + ATTN_CH + GMLP_CH
IN_CH = 2 * CONV_CH + 3 * ATTN_CH + 2 * GMLP_CH
D_FF = 2816
FFN_CONV_WIDTH = 3
LN_EPS = 1e-5

LANES = 128
COL_BLOCK = 512
COL_A, COL_Q, COL_K, COL_V, COL_C = range(IN_CH // COL_BLOCK)
QKV_Q, QKV_K, QKV_V = range(3)

ROW_TILE = 512
CONV_HALO = 32
CONV_ROW_CHUNK = 64
HEAD_PAIR = 2 * HEAD_DIM
N_PAIRS = ATTN_HEADS // 2
ATTN_TILE = ATTN_BLOCK * max(d for _, d in DILATED_PATTERNS)
ATTN_AHEAD = 1
ATTN_GROUP = 1
FF_CHUNK = 256
FFN_LOOKAHEAD = 1
FFN_SLOTS = FFN_LOOKAHEAD + 1
FFN_ROWS = 64
POST_TILES = 1
FFN_HALO = 8
LOG2_E = math.log2(math.e)
NEG_LOGIT = -1e30
VMEM_LIMIT_BYTES = 56 * 1024 * 1024

BF16 = jnp.bfloat16
F32 = jnp.float32


def _const_spec(shape):
    zeros = (0,) * len(shape)
    return pl.BlockSpec(shape, lambda *_: zeros, pipeline_mode=pl.Buffered(1))


def _params(*semantics):
    return pltpu.CompilerParams(dimension_semantics=semantics,
                                vmem_limit_bytes=VMEM_LIMIT_BYTES)


def _layernorm(x, g, b):
    mu = jnp.mean(x, axis=-1, keepdims=True)
    xc = x - mu
    var = jnp.mean(xc * xc, axis=-1, keepdims=True)
    return xc * lax.rsqrt(var + LN_EPS) * g + b


def _sigmoid(x):
    return 0.5 * jnp.tanh(0.5 * x) + 0.5


def _inproj_body(tiles_per_seq, x_ref, w_ref, b_ref, dww_ref, dwb_ref, cg_ref, cb_ref,
                 gg_ref, gb_ref, ws_ref, bs_ref, qkv_ref, o_ref, xb_ref, buf_ref, c_ref):
    first = (pl.program_id(0) % tiles_per_seq) == 0
    xb_ref[...] = x_ref[...].astype(BF16)

    @pl.when(pl.program_id(0) == 0)
    def _():
        buf_ref[...] = jnp.zeros_like(buf_ref)

    def project(col):
        cols = slice(col * COL_BLOCK, (col + 1) * COL_BLOCK)
        return jnp.dot(xb_ref[...], w_ref[:, cols], preferred_element_type=F32) + b_ref[:, cols]

    a = project(COL_A)
    glu = a[:, :CONV_CH] * _sigmoid(a[:, CONV_CH:])
    slabs = CONV_CH // LANES
    for s in range(slabs):
        lanes = slice(s * LANES, (s + 1) * LANES)
        prev_tail = buf_ref[s, pl.ds(2 * ROW_TILE, CONV_HALO, stride=2), :]
        buf_ref[s, pl.ds(0, CONV_HALO, stride=2), :] = jnp.where(first, 0.0, prev_tail)
        buf_ref[s, pl.ds(2 * CONV_HALO, ROW_TILE, stride=2), :] = glu[:, lanes]
    lead = CONV_HALO - (CONV_WIDTH - 1)

    def conv_rows(r0):
        accs = []
        for s in range(slabs):
            lanes = slice(s * LANES, (s + 1) * LANES)
            acc = jnp.broadcast_to(dwb_ref[:, lanes], (CONV_ROW_CHUNK, LANES))
            for k in range(CONV_WIDTH):
                window = pl.ds(2 * (r0 + lead + k), CONV_ROW_CHUNK, stride=2)
                acc = acc + dww_ref[k:k + 1, lanes] * buf_ref[s, window, :]
            accs.append(acc)
        y = _layernorm(jnp.concatenate(accs, axis=1), cg_ref[...], cb_ref[...])
        o_ref[r0:r0 + CONV_ROW_CHUNK, 0:CONV_CH] = (y * _sigmoid(y)).astype(BF16)

    c_ref[...] = project(COL_C)
    for j, col in enumerate((COL_Q, COL_K, COL_V)):
        qkv_ref[:, j * COL_BLOCK:(j + 1) * COL_BLOCK] = project(col)
    for r0 in range(0, ROW_TILE, CONV_ROW_CHUNK):
        conv_rows(r0)

    lane = lax.broadcasted_iota(jnp.int32, (1, GMLP_CH), 1)
    for ch in range(ROW_TILE // CHUNK):
        rows = slice(ch * CHUNK, (ch + 1) * CHUNK)
        c = c_ref[rows, :]
        vn = _layernorm(c[:, GMLP_CH:], gg_ref[...], gb_ref[...])
        stacked = jnp.concatenate(
            [jnp.where((lane // GMLP_GROUP_DIM) == g, vn, 0.0) for g in range(GMLP_GROUPS)],
            axis=0).astype(BF16)
        mixed = jnp.dot(ws_ref[...], stacked, preferred_element_type=F32) + bs_ref[...]
        o_ref[rows, CONV_CH:] = (c[:, :GMLP_CH] * mixed).astype(BF16)


def _inproj(x, seq, w, b, dww, dwb, cg, cb, gg, gb, ws_cat, bs_tab):
    n = x.shape[0]
    consts = [w, b, dww, dwb, cg, cb, gg, gb, ws_cat, bs_tab]
    row = lambda width: pl.BlockSpec((ROW_TILE, width), lambda i: (i, 0))
    return pl.pallas_call(
        functools.partial(_inproj_body, seq // ROW_TILE),
        grid=(n // ROW_TILE,),
        in_specs=[row(D_MODEL)] + [_const_spec(c.shape) for c in consts],
        out_specs=[row(3 * ATTN_CH), row(CONV_CH + GMLP_CH)],
        out_shape=[jax.ShapeDtypeStruct((n, 3 * ATTN_CH), F32),
                   jax.ShapeDtypeStruct((n, CONV_CH + GMLP_CH), BF16)],
        scratch_shapes=[pltpu.VMEM((ROW_TILE, D_MODEL), BF16),
                        pltpu.VMEM((CONV_CH // LANES, 2 * (CONV_HALO + ROW_TILE), LANES), F32),
                        pltpu.VMEM((ROW_TILE, 2 * GMLP_CH), F32)],
        compiler_params=_params("arbitrary"),
        name="inproj_conv_gmlp",
    )(x, *consts)


def _attn_scores(q2, k2, bias_lo, bias_hi):
    lo = lax.broadcasted_iota(jnp.int32, (1, HEAD_PAIR), 1) < HEAD_DIM
    nt = (((1,), (1,)), ((), ()))
    q2 = q2 * (HEAD_DIM ** -0.5 * LOG2_E)
    kb = k2.astype(BF16)
    ps, ms = [], []
    for keep, bias in ((lo, bias_lo), (jnp.logical_not(lo), bias_hi)):
        qh = jnp.where(keep, q2, 0.0).astype(BF16)
        s = lax.dot_general(qh, kb, nt, preferred_element_type=F32) + bias
        m = jnp.max(s, axis=-1, keepdims=True)
        ps.append(jnp.exp2(s - m).astype(BF16))
        ms.append(m)
    return jnp.concatenate(ps, axis=1), jnp.where(lo, ms[0], ms[1])


def _attn_values(p, v2):
    lo = lax.broadcasted_iota(jnp.int32, (1, HEAD_PAIR), 1) < HEAD_DIM
    shape = (2 * ATTN_BLOCK, HEAD_PAIR)
    ind_lo = jnp.broadcast_to(jnp.where(lo, 1.0, 0.0), shape).astype(BF16)
    ind_hi = jnp.broadcast_to(jnp.where(lo, 0.0, 1.0), shape).astype(BF16)
    vb = v2.astype(BF16)
    zero = jnp.zeros_like(vb)
    v_stack = jnp.concatenate(
        [jnp.concatenate([jnp.where(lo, vb, zero), ind_lo], axis=1),
         jnp.concatenate([jnp.where(lo, zero, vb), ind_hi], axis=1)], axis=0)
    ol = jnp.dot(p, v_stack, preferred_element_type=F32)
    return ol[:, :HEAD_PAIR], ol[:, HEAD_PAIR:]


def _attn_body(q_ref, kc_ref, kp_ref, vc_ref, vp_ref, bias_ref, out_ref,
               p_scr, o_scr, m_scr, l_scr):
    pair = pl.program_id(1)
    edge = jnp.where(pl.program_id(2) == 0, 0, 1)

    def rows(start, n, d):
        return pl.ds(start, n) if d == 1 else pl.ds(start, n, stride=d)

    def window(cur_ref, prev_ref, d, start, span, seq_edge):
        if seq_edge:
            return jnp.concatenate(
                [prev_ref[rows(ATTN_TILE + start - span, ATTN_BLOCK, d), :],
                 cur_ref[rows(start, ATTN_BLOCK, d), :]], axis=0)
        return cur_ref[rows(start - span, 2 * ATTN_BLOCK, d), :]

    order = sorted(range(len(DILATED_PATTERNS)), key=lambda p: -DILATED_PATTERNS[p][1])
    assert DILATED_PATTERNS[order[-1]][1] == 1
    units = []
    for p in order:
        d = DILATED_PATTERNS[p][1]
        span = d * ATTN_BLOCK
        for r in range(d):
            for j in range(ATTN_TILE // span):
                units.append((p, d, r + j * span, span, j == 0))
    groups = [units[g:g + ATTN_GROUP] for g in range(0, len(units), ATTN_GROUP)]

    def merge(start):
        r = slice(start, start + ATTN_BLOCK)
        ms = [m_scr[p, r, :] for p in range(len(DILATED_PATTERNS))]
        big = functools.reduce(jnp.maximum, ms)
        ws = [jnp.exp2(m - big) for m in ms]
        num = sum(w * o_scr[p, r, :] for p, w in enumerate(ws))
        den = sum(w * l_scr[p, r, :] for p, w in enumerate(ws))
        out_ref[r, :] = (num / den).astype(BF16)

    def scores(group, slot):
        for t, (p, d, start, span, seq_edge) in enumerate(group):
            sel = edge if seq_edge else 1
            probs, m = _attn_scores(q_ref[rows(start, ATTN_BLOCK, d), :],
                                    window(kc_ref, kp_ref, d, start, span, seq_edge),
                                    bias_ref[p, sel, 2 * pair], bias_ref[p, sel, 2 * pair + 1])
            p_scr[slot, t] = probs
            m_scr[p, rows(start, ATTN_BLOCK, d), :] = m

    def values(group, slot):
        for t, (p, d, start, span, seq_edge) in enumerate(group):
            o, l = _attn_values(p_scr[slot, t], window(vc_ref, vp_ref, d, start, span, seq_edge))
            o_scr[p, rows(start, ATTN_BLOCK, d), :] = o
            l_scr[p, rows(start, ATTN_BLOCK, d), :] = l
            if d == 1:
                merge(start)

    slots = ATTN_AHEAD + 1
    for g in range(ATTN_AHEAD):
        scores(groups[g], g % slots)
    for g in range(len(groups)):
        if g + ATTN_AHEAD < len(groups):
            scores(groups[g + ATTN_AHEAD], (g + ATTN_AHEAD) % slots)
        values(groups[g], g % slots)


def _attention(h, batch, seq, bias):
    n = h.shape[0]
    tiles = seq // ATTN_TILE
    lane_blocks = COL_BLOCK // LANES

    def cur(col):
        return pl.BlockSpec((ATTN_TILE, LANES),
                            lambda b, hp, t: (b * tiles + t, col * lane_blocks + hp))

    def prev(col):
        return pl.BlockSpec((ATTN_TILE, LANES),
                            lambda b, hp, t: (b * tiles + jnp.maximum(t - 1, 0),
                                              col * lane_blocks + hp))

    return pl.pallas_call(
        _attn_body,
        grid=(batch, N_PAIRS, tiles),
        in_specs=[cur(QKV_Q), cur(QKV_K), prev(QKV_K), cur(QKV_V), prev(QKV_V),
                  _const_spec(bias.shape)],
        out_specs=pl.BlockSpec((ATTN_TILE, LANES), lambda b, hp, t: (b * tiles + t, hp)),
        out_shape=jax.ShapeDtypeStruct((n, ATTN_CH), BF16),
        scratch_shapes=[pltpu.VMEM((ATTN_AHEAD + 1, ATTN_GROUP, ATTN_BLOCK, 4 * ATTN_BLOCK), BF16),
                        pltpu.VMEM((len(DILATED_PATTERNS), ATTN_TILE, LANES), F32),
                        pltpu.VMEM((len(DILATED_PATTERNS), ATTN_TILE, LANES), F32),
                        pltpu.VMEM((len(DILATED_PATTERNS), ATTN_TILE, LANES), F32)],
        compiler_params=_params("parallel", "parallel", "arbitrary"),
        name="dilated_attn",
    )(h, h, h, h, h, bias)


def _t5_bucket(dist):
    max_exact = N_BUCKETS // 2
    d = np.maximum(dist, 1).astype(np.float64)
    large = max_exact + (np.log(d / max_exact) / math.log(MAX_DISTANCE / max_exact)
                         * (N_BUCKETS - max_exact)).astype(np.int32)
    large = np.minimum(large, N_BUCKETS - 1)
    return np.where(dist < max_exact, dist, large).astype(np.int32)


def _attn_bias(rel_table, window, dilation):
    n_win = window // dilation
    assert n_win <= ATTN_BLOCK
    heads = rel_table.shape[1]
    onehot = np.eye(N_BUCKETS, dtype=np.float32)[_t5_bucket(np.arange(n_win + 1) * dilation)]
    by_dist = jnp.sum(onehot[:, :, None] * rel_table.astype(F32)[None], axis=1)
    period = 3 * ATTN_BLOCK + 1
    u = jnp.concatenate([jnp.full((heads, ATTN_BLOCK - n_win), NEG_LOGIT, F32), by_dist[::-1].T,
                         jnp.full((heads, period - ATTN_BLOCK - 1), NEG_LOGIT, F32)], axis=1)
    toeplitz = jnp.tile(u, (1, ATTN_BLOCK))[:, :ATTN_BLOCK * (period - 1)]
    main = toeplitz.reshape(heads, ATTN_BLOCK, period - 1)[:, :, :2 * ATTN_BLOCK]
    kj = np.arange(2 * ATTN_BLOCK)[None, None, :]
    first = jnp.where(jnp.asarray(kj >= ATTN_BLOCK), main, NEG_LOGIT)
    return jnp.stack([first, main]) * LOG2_E


def _post_body(alpha, tiles_per_seq, ac_ref, attn_ref, x_ref, wo_ref, bo_ref, g1_ref, b1_ref,
               wup_ref, bup_ref, cw_ref, cb_ref, wdn_ref, bdn_ref, g2_ref, b2_ref, y_ref,
               x1_ref, xb_ref, h_buf, act_ref, tail_ref):
    mix_half = CONV_CH + GMLP_CH
    for t in range(POST_TILES):
        rows = slice(t * ROW_TILE, (t + 1) * ROW_TILE)
        mix = (jnp.dot(ac_ref[rows, :], wo_ref[0:mix_half, :], preferred_element_type=F32)
               + jnp.dot(attn_ref[rows, :], wo_ref[mix_half:, :], preferred_element_type=F32)
               + bo_ref[...])
        x1 = _layernorm(alpha * x_ref[rows, :] + mix, g1_ref[...], b1_ref[...])
        x1_ref[t] = x1
        xb_ref[t] = x1.astype(BF16)

    tile0 = pl.program_id(0) * POST_TILES
    firsts = [((tile0 + t) % tiles_per_seq) == 0 for t in range(POST_TILES)]
    half_slabs = FF_CHUNK // LANES
    slabs = 2 * half_slabs
    n_chunks = D_FF // FF_CHUNK

    def chunk_cols(c, half):
        start = half * D_FF + c * FF_CHUNK
        return slice(start, start + FF_CHUNK)

    steps = [(t, c) for t in range(POST_TILES) for c in range(n_chunks)]

    def up(n):
        t, c = steps[n]
        for half in range(2):
            cols = chunk_cols(c, half)
            h = jnp.dot(xb_ref[t], wup_ref[:, cols], preferred_element_type=F32)
            halo = jnp.where(firsts[t], -bup_ref[:, cols], tail_ref[:, cols])
            tail_ref[:, cols] = h[ROW_TILE - FFN_HALO:, :]
            for s in range(half_slabs):
                lanes = slice(s * LANES, (s + 1) * LANES)
                buf = h_buf.at[n % FFN_SLOTS, half * half_slabs + s]
                buf[pl.ds(0, FFN_HALO, stride=2), :] = halo[:, lanes]
                buf[pl.ds(2 * FFN_HALO, ROW_TILE, stride=2), :] = h[:, lanes]

    def conv(n, s, r0):
        start = chunk_cols(steps[n][1], s // half_slabs).start + (s % half_slabs) * LANES
        wcols = slice(start, start + LANES)
        y = cb_ref[:, wcols]
        for k in range(FFN_CONV_WIDTH):
            shift = FFN_CONV_WIDTH - 1 - k
            y = y + cw_ref[k:k + 1, wcols] * h_buf[n % FFN_SLOTS, s, pl.ds(
                2 * (FFN_HALO + r0 - shift), FFN_ROWS, stride=2), :]
        return y

    for n in range(FFN_LOOKAHEAD):
        up(n)
    acc = None
    for n, (t, c) in enumerate(steps):
        if n + FFN_LOOKAHEAD < len(steps):
            up(n + FFN_LOOKAHEAD)
        for r0 in range(0, ROW_TILE, FFN_ROWS):
            for s in range(half_slabs):
                gate, val = conv(n, s, r0), conv(n, s + half_slabs, r0)
                act_ref[n % 2, r0:r0 + FFN_ROWS, s * LANES:(s + 1) * LANES] = (
                    gate * _sigmoid(gate) * val).astype(BF16)
        if c == 0:
            acc = alpha * x1_ref[t] + bdn_ref[...]
        acc = acc + jnp.dot(act_ref[n % 2], wdn_ref[c * FF_CHUNK:(c + 1) * FF_CHUNK, :],
                            preferred_element_type=F32)
        if c == n_chunks - 1:
            y_ref[t * ROW_TILE:(t + 1) * ROW_TILE, :] = _layernorm(acc, g2_ref[...], b2_ref[...])


def _post(alpha, ac, attn, x, seq, wo, bo, g1, b1, wup, bup, cw, cb, wdn, bdn, g2, b2):
    n = x.shape[0]
    rows = POST_TILES * ROW_TILE
    row = lambda width: pl.BlockSpec((rows, width), lambda i: (i, 0))
    consts = [wo, bo, g1, b1, wup, bup, cw, cb, wdn, bdn, g2, b2]
    return pl.pallas_call(
        functools.partial(_post_body, alpha, seq // ROW_TILE),
        grid=(n // rows,),
        in_specs=[row(CONV_CH + GMLP_CH), row(ATTN_CH), row(D_MODEL)]
                 + [_const_spec(c.shape) for c in consts],
        out_specs=row(D_MODEL),
        out_shape=jax.ShapeDtypeStruct((n, D_MODEL), F32),
        scratch_shapes=[pltpu.VMEM((POST_TILES, ROW_TILE, D_MODEL), F32),
                        pltpu.VMEM((POST_TILES, ROW_TILE, D_MODEL), BF16),
                        pltpu.VMEM((FFN_SLOTS, 2 * FF_CHUNK // LANES, 2 * (FFN_HALO + ROW_TILE),
                                    LANES), F32),
                        pltpu.VMEM((2, ROW_TILE, FF_CHUNK), BF16),
                        pltpu.VMEM((FFN_HALO, 2 * D_FF), F32)],
        compiler_params=_params("arbitrary"),
        name="outproj_convffn",
    )(ac, attn, x, *consts)


def kernel(x, w_in, b_in, conv_dw_w, conv_dw_b, conv_ln_g, conv_ln_b, rel_bias_table, gmlp_ln_g, gmlp_ln_b, gmlp_w_s, gmlp_b_s, w_out, b_out, ln1_g, ln1_b, ffn_w_up, ffn_b_up, ffn_conv_w, ffn_conv_b, ffn_w_down, ffn_b_down, ln2_g, ln2_b):
    batch, seq, _ = x.shape
    depth = w_in.shape[0]
    alpha = (2.0 * depth) ** 0.25
    n = batch * seq
    row2d = lambda v: v.reshape(1, -1)
    bias = jnp.stack([_attn_bias(rel_bias_table, w, d) for (w, d) in DILATED_PATTERNS])

    xf = x.reshape(n, D_MODEL)
    for l in range(depth):
        ws_cat = jnp.transpose(jnp.tril(gmlp_w_s[l]), (1, 0, 2)).reshape(CHUNK, GMLP_GROUPS * CHUNK)
        bs_tab = jnp.repeat(gmlp_b_s[l].T, GMLP_GROUP_DIM, axis=1)
        qkv, ac = _inproj(xf, seq, w_in[l].astype(BF16), row2d(b_in[l]), conv_dw_w[l],
                          row2d(conv_dw_b[l]), row2d(conv_ln_g[l]), row2d(conv_ln_b[l]),
                          row2d(gmlp_ln_g[l]), row2d(gmlp_ln_b[l]), ws_cat.astype(BF16), bs_tab)
        attn = _attention(qkv, batch, seq, bias)
        wo = w_out[l].astype(BF16)
        wo = jnp.concatenate([wo[:CONV_CH], wo[CONV_CH + ATTN_CH:], wo[CONV_CH:CONV_CH + ATTN_CH]])
        xf = _post(alpha, ac, attn, xf, seq, wo, row2d(b_out[l]), row2d(ln1_g[l]),
                   row2d(ln1_b[l]), ffn_w_up[l].astype(BF16), row2d(ffn_b_up[l]), ffn_conv_w[l],
                   row2d(ffn_conv_b[l] + ffn_b_up[l] * jnp.sum(ffn_conv_w[l], axis=0)),
                   ffn_w_down[l].astype(BF16), row2d(ffn_b_down[l]),
                   row2d(ln2_g[l]), row2d(ln2_b[l]))
    return xf.reshape(batch, seq, D_MODEL)
```

```python
import functools
import math

import jax
import jax.numpy as jnp
import numpy as np
from jax import lax
from jax.experimental import pallas as pl
from jax.experimental.pallas import tpu as pltpu

D_MODEL = 1024
HEAD_DIM = 64
CONV_CH = 256
CONV_WIDTH = 31
ATTN_HEADS = 8
ATTN_CH = ATTN_HEADS * HEAD_DIM
DILATED_PATTERNS = ((128, 1), (512, 4), (2048, 16))
ATTN_BLOCK = 128
N_BUCKETS = 32
MAX_DISTANCE = 2048
GMLP_CH = 256
GMLP_GROUPS = 4
GMLP_GROUP_DIM = GMLP_CH // GMLP_GROUPS
CHUNK = 128
MIX_CH = CONV_CH + ATTN_CH + GMLP_CH
IN_CH = 2 * CONV_CH + 3 * ATTN_CH + 2 * GMLP_CH
D_FF = 2816
FFN_CONV_WIDTH = 3
LN_EPS = 1e-5

LANES = 128
COL_BLOCK = 512
COL_A, COL_Q, COL_K, COL_V, COL_C = range(IN_CH // COL_BLOCK)
QKV_Q, QKV_K, QKV_V = range(3)

ROW_TILE = 512
CONV_HALO = 32
CONV_ROW_CHUNK = 64
HEAD_PAIR = 2 * HEAD_DIM
N_PAIRS = ATTN_HEADS // 2
ATTN_TILE = ATTN_BLOCK * max(d for _, d in DILATED_PATTERNS)
ATTN_AHEAD = 1
ATTN_GROUP = 1
FF_CHUNK = 256
FFN_LOOKAHEAD = 2
FFN_SLOTS = FFN_LOOKAHEAD + 1
FFN_ROWS = 64
POST_TILES = 1
FFN_HALO = 8
LOG2_E = math.log2(math.e)
NEG_LOGIT = -1e30
VMEM_LIMIT_BYTES = 56 * 1024 * 1024

BF16 = jnp.bfloat16
F32 = jnp.float32


def _const_spec(shape):
    zeros = (0,) * len(shape)
    return pl.BlockSpec(shape, lambda *_: zeros, pipeline_mode=pl.Buffered(1))


def _params(*semantics):
    return pltpu.CompilerParams(dimension_semantics=semantics,
                                vmem_limit_bytes=VMEM_LIMIT_BYTES)


def _layernorm(x, g, b):
    mu = jnp.mean(x, axis=-1, keepdims=True)
    xc = x - mu
    var = jnp.mean(xc * xc, axis=-1, keepdims=True)
    return xc * lax.rsqrt(var + LN_EPS) * g + b


def _sigmoid(x):
    return 0.5 * jnp.tanh(0.5 * x) + 0.5


def _inproj_body(tiles_per_seq, x_ref, w_ref, b_ref, dww_ref, dwb_ref, cg_ref, cb_ref,
                 gg_ref, gb_ref, ws_ref, bs_ref, qkv_ref, o_ref, xb_ref, buf_ref, c_ref):
    first = (pl.program_id(0) % tiles_per_seq) == 0
    xb_ref[...] = x_ref[...].astype(BF16)

    @pl.when(pl.program_id(0) == 0)
    def _():
        buf_ref[...] = jnp.zeros_like(buf_ref)

    def project(col):
        cols = slice(col * COL_BLOCK, (col + 1) * COL_BLOCK)
        return jnp.dot(xb_ref[...], w_ref[:, cols], preferred_element_type=F32) + b_ref[:, cols]

    a = project(COL_A)
    glu = a[:, :CONV_CH] * _sigmoid(a[:, CONV_CH:])
    slabs = CONV_CH // LANES
    for s in range(slabs):
        lanes = slice(s * LANES, (s + 1) * LANES)
        prev_tail = buf_ref[s, pl.ds(2 * ROW_TILE, CONV_HALO, stride=2), :]
        buf_ref[s, pl.ds(0, CONV_HALO, stride=2), :] = jnp.where(first, 0.0, prev_tail)
        buf_ref[s, pl.ds(2 * CONV_HALO, ROW_TILE, stride=2), :] = glu[:, lanes]
    lead = CONV_HALO - (CONV_WIDTH - 1)

    def conv_rows(r0):
        accs = []
        for s in range(slabs):
            lanes = slice(s * LANES, (s + 1) * LANES)
            acc = jnp.broadcast_to(dwb_ref[:, lanes], (CONV_ROW_CHUNK, LANES))
            for k in range(CONV_WIDTH):
                window = pl.ds(2 * (r0 + lead + k), CONV_ROW_CHUNK, stride=2)
                acc = acc + dww_ref[k:k + 1, lanes] * buf_ref[s, window, :]
            accs.append(acc)
        y = _layernorm(jnp.concatenate(accs, axis=1), cg_ref[...], cb_ref[...])
        o_ref[r0:r0 + CONV_ROW_CHUNK, 0:CONV_CH] = (y * _sigmoid(y)).astype(BF16)

    c_ref[...] = project(COL_C)
    for j, col in enumerate((COL_Q, COL_K, COL_V)):
        qkv_ref[:, j * COL_BLOCK:(j + 1) * COL_BLOCK] = project(col)
    for r0 in range(0, ROW_TILE, CONV_ROW_CHUNK):
        conv_rows(r0)

    lane = lax.broadcasted_iota(jnp.int32, (1, GMLP_CH), 1)
    for ch in range(ROW_TILE // CHUNK):
        rows = slice(ch * CHUNK, (ch + 1) * CHUNK)
        c = c_ref[rows, :]
        vn = _layernorm(c[:, GMLP_CH:], gg_ref[...], gb_ref[...])
        stacked = jnp.concatenate(
            [jnp.where((lane // GMLP_GROUP_DIM) == g, vn, 0.0) for g in range(GMLP_GROUPS)],
            axis=0).astype(BF16)
        mixed = jnp.dot(ws_ref[...], stacked, preferred_element_type=F32) + bs_ref[...]
        o_ref[rows, CONV_CH:] = (c[:, :GMLP_CH] * mixed).astype(BF16)


def _inproj(x, seq, w, b, dww, dwb, cg, cb, gg, gb, ws_cat, bs_tab):
    n = x.shape[0]
    consts = [w, b, dww, dwb, cg, cb, gg, gb, ws_cat, bs_tab]
    row = lambda width: pl.BlockSpec((ROW_TILE, width), lambda i: (i, 0))
    return pl.pallas_call(
        functools.partial(_inproj_body, seq // ROW_TILE),
        grid=(n // ROW_TILE,),
        in_specs=[row(D_MODEL)] + [_const_spec(c.shape) for c in consts],
        out_specs=[row(3 * ATTN_CH), row(CONV_CH + GMLP_CH)],
        out_shape=[jax.ShapeDtypeStruct((n, 3 * ATTN_CH), F32),
                   jax.ShapeDtypeStruct((n, CONV_CH + GMLP_CH), BF16)],
        scratch_shapes=[pltpu.VMEM((ROW_TILE, D_MODEL), BF16),
                        pltpu.VMEM((CONV_CH // LANES, 2 * (CONV_HALO + ROW_TILE), LANES), F32),
                        pltpu.VMEM((ROW_TILE, 2 * GMLP_CH), F32)],
        compiler_params=_params("arbitrary"),
        name="inproj_conv_gmlp",
    )(x, *consts)


def _attn_scores(q2, k2, bias_lo, bias_hi):
    lo = lax.broadcasted_iota(jnp.int32, (1, HEAD_PAIR), 1) < HEAD_DIM
    nt = (((1,), (1,)), ((), ()))
    q2 = q2 * (HEAD_DIM ** -0.5 * LOG2_E)
    kb = k2.astype(BF16)
    ps, ms = [], []
    for keep, bias in ((lo, bias_lo), (jnp.logical_not(lo), bias_hi)):
        qh = jnp.where(keep, q2, 0.0).astype(BF16)
        s = lax.dot_general(qh, kb, nt, preferred_element_type=F32) + bias
        m = jnp.max(s, axis=-1, keepdims=True)
        ps.append(jnp.exp2(s - m).astype(BF16))
        ms.append(m)
    return jnp.concatenate(ps, axis=1), jnp.where(lo, ms[0], ms[1])


def _attn_values(p, v2):
    lo = lax.broadcasted_iota(jnp.int32, (1, HEAD_PAIR), 1) < HEAD_DIM
    shape = (2 * ATTN_BLOCK, HEAD_PAIR)
    ind_lo = jnp.broadcast_to(jnp.where(lo, 1.0, 0.0), shape).astype(BF16)
    ind_hi = jnp.broadcast_to(jnp.where(lo, 0.0, 1.0), shape).astype(BF16)
    vb = v2.astype(BF16)
    zero = jnp.zeros_like(vb)
    v_stack = jnp.concatenate(
        [jnp.concatenate([jnp.where(lo, vb, zero), ind_lo], axis=1),
         jnp.concatenate([jnp.where(lo, zero, vb), ind_hi], axis=1)], axis=0)
    ol = jnp.dot(p, v_stack, preferred_element_type=F32)
    return ol[:, :HEAD_PAIR], ol[:, HEAD_PAIR:]


def _attn_body(q_ref, kc_ref, kp_ref, vc_ref, vp_ref, bias_ref, out_ref,
               p_scr, o_scr, m_scr, l_scr):
    pair = pl.program_id(1)
    edge = jnp.where(pl.program_id(2) == 0, 0, 1)

    def rows(start, n, d):
        return pl.ds(start, n) if d == 1 else pl.ds(start, n, stride=d)

    def window(cur_ref, prev_ref, d, start, span, seq_edge):
        if seq_edge:
            return jnp.concatenate(
                [prev_ref[rows(ATTN_TILE + start - span, ATTN_BLOCK, d), :],
                 cur_ref[rows(start, ATTN_BLOCK, d), :]], axis=0)
        return cur_ref[rows(start - span, 2 * ATTN_BLOCK, d), :]

    order = sorted(range(len(DILATED_PATTERNS)), key=lambda p: -DILATED_PATTERNS[p][1])
    assert DILATED_PATTERNS[order[-1]][1] == 1
    units = []
    for p in order:
        d = DILATED_PATTERNS[p][1]
        span = d * ATTN_BLOCK
        for r in range(d):
            for j in range(ATTN_TILE // span):
                units.append((p, d, r + j * span, span, j == 0))
    groups = [units[g:g + ATTN_GROUP] for g in range(0, len(units), ATTN_GROUP)]

    def merge(start):
        r = slice(start, start + ATTN_BLOCK)
        ms = [m_scr[p, r, :] for p in range(len(DILATED_PATTERNS))]
        big = functools.reduce(jnp.maximum, ms)
        ws = [jnp.exp2(m - big) for m in ms]
        num = sum(w * o_scr[p, r, :] for p, w in enumerate(ws))
        den = sum(w * l_scr[p, r, :] for p, w in enumerate(ws))
        out_ref[r, :] = (num / den).astype(BF16)

    def scores(group, slot):
        for t, (p, d, start, span, seq_edge) in enumerate(group):
            sel = edge if seq_edge else 1
            probs, m = _attn_scores(q_ref[rows(start, ATTN_BLOCK, d), :],
                                    window(kc_ref, kp_ref, d, start, span, seq_edge),
                                    bias_ref[p, sel, 2 * pair], bias_ref[p, sel, 2 * pair + 1])
            p_scr[slot, t] = probs
            m_scr[p, rows(start, ATTN_BLOCK, d), :] = m

    def values(group, slot):
        for t, (p, d, start, span, seq_edge) in enumerate(group):
            o, l = _attn_values(p_scr[slot, t], window(vc_ref, vp_ref, d, start, span, seq_edge))
            o_scr[p, rows(start, ATTN_BLOCK, d), :] = o
            l_scr[p, rows(start, ATTN_BLOCK, d), :] = l
            if d == 1:
                merge(start)

    slots = ATTN_AHEAD + 1
    for g in range(ATTN_AHEAD):
        scores(groups[g], g % slots)
    for g in range(len(groups)):
        if g + ATTN_AHEAD < len(groups):
            scores(groups[g + ATTN_AHEAD], (g + ATTN_AHEAD) % slots)
        values(groups[g], g % slots)


def _attention(h, batch, seq, bias):
    n = h.shape[0]
    tiles = seq // ATTN_TILE
    lane_blocks = COL_BLOCK // LANES

    def cur(col):
        return pl.BlockSpec((ATTN_TILE, LANES),
                            lambda b, hp, t: (b * tiles + t, col * lane_blocks + hp))

    def prev(col):
        return pl.BlockSpec((ATTN_TILE, LANES),
                            lambda b, hp, t: (b * tiles + jnp.maximum(t - 1, 0),
                                              col * lane_blocks + hp))

    return pl.pallas_call(
        _attn_body,
        grid=(batch, N_PAIRS, tiles),
        in_specs=[cur(QKV_Q), cur(QKV_K), prev(QKV_K), cur(QKV_V), prev(QKV_V),
                  _const_spec(bias.shape)],
        out_specs=pl.BlockSpec((ATTN_TILE, LANES), lambda b, hp, t: (b * tiles + t, hp)),
        out_shape=jax.ShapeDtypeStruct((n, ATTN_CH), BF16),
        scratch_shapes=[pltpu.VMEM((ATTN_AHEAD + 1, ATTN_GROUP, ATTN_BLOCK, 4 * ATTN_BLOCK), BF16),
                        pltpu.VMEM((len(DILATED_PATTERNS), ATTN_TILE, LANES), F32),
                        pltpu.VMEM((len(DILATED_PATTERNS), ATTN_TILE, LANES), F32),
                        pltpu.VMEM((len(DILATED_PATTERNS), ATTN_TILE, LANES), F32)],
        compiler_params=_params("parallel", "parallel", "arbitrary"),
        name="dilated_attn",
    )(h, h, h, h, h, bias)


def _t5_bucket(dist):
    max_exact = N_BUCKETS // 2
    d = np.maximum(dist, 1).astype(np.float64)
    large = max_exact + (np.log(d / max_exact) / math.log(MAX_DISTANCE / max_exact)
                         * (N_BUCKETS - max_exact)).astype(np.int32)
    large = np.minimum(large, N_BUCKETS - 1)
    return np.where(dist < max_exact, dist, large).astype(np.int32)


def _attn_bias(rel_table, window, dilation):
    n_win = window // dilation
    assert n_win <= ATTN_BLOCK
    heads = rel_table.shape[1]
    onehot = np.eye(N_BUCKETS, dtype=np.float32)[_t5_bucket(np.arange(n_win + 1) * dilation)]
    by_dist = jnp.sum(onehot[:, :, None] * rel_table.astype(F32)[None], axis=1)
    period = 3 * ATTN_BLOCK + 1
    u = jnp.concatenate([jnp.full((heads, ATTN_BLOCK - n_win), NEG_LOGIT, F32), by_dist[::-1].T,
                         jnp.full((heads, period - ATTN_BLOCK - 1), NEG_LOGIT, F32)], axis=1)
    toeplitz = jnp.tile(u, (1, ATTN_BLOCK))[:, :ATTN_BLOCK * (period - 1)]
    main = toeplitz.reshape(heads, ATTN_BLOCK, period - 1)[:, :, :2 * ATTN_BLOCK]
    kj = np.arange(2 * ATTN_BLOCK)[None, None, :]
    first = jnp.where(jnp.asarray(kj >= ATTN_BLOCK), main, NEG_LOGIT)
    return jnp.stack([first, main]) * LOG2_E


def _post_body(alpha, tiles_per_seq, ac_ref, attn_ref, x_ref, wo_ref, bo_ref, g1_ref, b1_ref,
               wup_ref, bup_ref, cw_ref, cb_ref, wdn_ref, bdn_ref, g2_ref, b2_ref, y_ref,
               x1_ref, xb_ref, h_buf, act_ref, tail_ref):
    gm_row = CONV_CH + ATTN_CH
    for t in range(POST_TILES):
        rows = slice(t * ROW_TILE, (t + 1) * ROW_TILE)
        mix = (jnp.dot(ac_ref[rows, :CONV_CH], wo_ref[0:CONV_CH, :], preferred_element_type=F32)
               + jnp.dot(attn_ref[rows, :], wo_ref[CONV_CH:gm_row, :], preferred_element_type=F32)
               + jnp.dot(ac_ref[rows, CONV_CH:], wo_ref[gm_row:, :], preferred_element_type=F32)
               + bo_ref[...])
        x1 = _layernorm(alpha * x_ref[rows, :] + mix, g1_ref[...], b1_ref[...])
        x1_ref[t] = x1
        xb_ref[t] = x1.astype(BF16)

    tile0 = pl.program_id(0) * POST_TILES
    firsts = [((tile0 + t) % tiles_per_seq) == 0 for t in range(POST_TILES)]
    half_slabs = FF_CHUNK // LANES
    slabs = 2 * half_slabs
    n_chunks = D_FF // FF_CHUNK

    def chunk_cols(c, half):
        start = half * D_FF + c * FF_CHUNK
        return slice(start, start + FF_CHUNK)

    steps = [(t, c) for t in range(POST_TILES) for c in range(n_chunks)]

    def up(n):
        t, c = steps[n]
        for half in range(2):
            cols = chunk_cols(c, half)
            h = jnp.dot(xb_ref[t], wup_ref[:, cols], preferred_element_type=F32)
            halo = jnp.where(firsts[t], -bup_ref[:, cols], tail_ref[:, cols])
            tail_ref[:, cols] = h[ROW_TILE - FFN_HALO:, :]
            for s in range(half_slabs):
                lanes = slice(s * LANES, (s + 1) * LANES)
                buf = h_buf.at[n % FFN_SLOTS, half * half_slabs + s]
                buf[pl.ds(0, FFN_HALO, stride=2), :] = halo[:, lanes]
                buf[pl.ds(2 * FFN_HALO, ROW_TILE, stride=2), :] = h[:, lanes]

    def conv(n, s, r0):
        start = chunk_cols(steps[n][1], s // half_slabs).start + (s % half_slabs) * LANES
        wcols = slice(start, start + LANES)
        y = cb_ref[:, wcols]
        for k in range(FFN_CONV_WIDTH):
            shift = FFN_CONV_WIDTH - 1 - k
            y = y + cw_ref[k:k + 1, wcols] * h_buf[n % FFN_SLOTS, s, pl.ds(
                2 * (FFN_HALO + r0 - shift), FFN_ROWS, stride=2), :]
        return y

    for n in range(FFN_LOOKAHEAD):
        up(n)
    acc = None
    for n, (t, c) in enumerate(steps):
        if n + FFN_LOOKAHEAD < len(steps):
            up(n + FFN_LOOKAHEAD)
        for r0 in range(0, ROW_TILE, FFN_ROWS):
            for s in range(half_slabs):
                gate, val = conv(n, s, r0), conv(n, s + half_slabs, r0)
                act_ref[n % FFN_SLOTS, r0:r0 + FFN_ROWS, s * LANES:(s + 1) * LANES] = (
                    gate * _sigmoid(gate) * val).astype(BF16)
        if c == 0:
            acc = alpha * x1_ref[t] + bdn_ref[...]
        acc = acc + jnp.dot(act_ref[n % FFN_SLOTS], wdn_ref[c * FF_CHUNK:(c + 1) * FF_CHUNK, :],
                            preferred_element_type=F32)
        if c == n_chunks - 1:
            y_ref[t * ROW_TILE:(t + 1) * ROW_TILE, :] = _layernorm(acc, g2_ref[...], b2_ref[...])


def _post(alpha, ac, attn, x, seq, wo, bo, g1, b1, wup, bup, cw, cb, wdn, bdn, g2, b2):
    n = x.shape[0]
    rows = POST_TILES * ROW_TILE
    row = lambda width: pl.BlockSpec((rows, width), lambda i: (i, 0))
    consts = [wo, bo, g1, b1, wup, bup, cw, cb, wdn, bdn, g2, b2]
    return pl.pallas_call(
        functools.partial(_post_body, alpha, seq // ROW_TILE),
        grid=(n // rows,),
        in_specs=[row(CONV_CH + GMLP_CH), row(ATTN_CH), row(D_MODEL)]
                 + [_const_spec(c.shape) for c in consts],
        out_specs=row(D_MODEL),
        out_shape=jax.ShapeDtypeStruct((n, D_MODEL), F32),
        scratch_shapes=[pltpu.VMEM((POST_TILES, ROW_TILE, D_MODEL), F32),
                        pltpu.VMEM((POST_TILES, ROW_TILE, D_MODEL), BF16),
                        pltpu.VMEM((FFN_SLOTS, 2 * FF_CHUNK // LANES, 2 * (FFN_HALO + ROW_TILE),
                                    LANES), F32),
                        pltpu.VMEM((FFN_SLOTS, ROW_TILE, FF_CHUNK), BF16),
                        pltpu.VMEM((FFN_HALO, 2 * D_FF), F32)],
        compiler_params=_params("arbitrary"),
        name="outproj_convffn",
    )(ac, attn, x, *consts)


def kernel(x, w_in, b_in, conv_dw_w, conv_dw_b, conv_ln_g, conv_ln_b, rel_bias_table, gmlp_ln_g, gmlp_ln_b, gmlp_w_s, gmlp_b_s, w_out, b_out, ln1_g, ln1_b, ffn_w_up, ffn_b_up, ffn_conv_w, ffn_conv_b, ffn_w_down, ffn_b_down, ln2_g, ln2_b):
    batch, seq, _ = x.shape
    depth = w_in.shape[0]
    alpha = (2.0 * depth) ** 0.25
    n = batch * seq
    row2d = lambda v: v.reshape(1, -1)
    bias = jnp.stack([_attn_bias(rel_bias_table, w, d) for (w, d) in DILATED_PATTERNS])

    xf = x.reshape(n, D_MODEL)
    for l in range(depth):
        ws_cat = jnp.transpose(jnp.tril(gmlp_w_s[l]), (1, 0, 2)).reshape(CHUNK, GMLP_GROUPS * CHUNK)
        bs_tab = jnp.repeat(gmlp_b_s[l].T, GMLP_GROUP_DIM, axis=1)
        qkv, ac = _inproj(xf, seq, w_in[l].astype(BF16), row2d(b_in[l]), conv_dw_w[l],
                          row2d(conv_dw_b[l]), row2d(conv_ln_g[l]), row2d(conv_ln_b[l]),
                          row2d(gmlp_ln_g[l]), row2d(gmlp_ln_b[l]), ws_cat.astype(BF16), bs_tab)
        attn = _attention(qkv, batch, seq, bias)
        xf = _post(alpha, ac, attn, xf, seq, w_out[l].astype(BF16), row2d(b_out[l]), row2d(ln1_g[l]),
                   row2d(ln1_b[l]), ffn_w_up[l].astype(BF16), row2d(ffn_b_up[l]), ffn_conv_w[l],
                   row2d(ffn_conv_b[l] + ffn_b_up[l] * jnp.sum(ffn_conv_w[l], axis=0)),
                   ffn_w_down[l].astype(BF16), row2d(ffn_b_down[l]),
                   row2d(ln2_g[l]), row2d(ln2_b[l]))
    return xf.reshape(batch, seq, D_MODEL)
```

```python
import functools
import math

import jax
import jax.numpy as jnp
import numpy as np
from jax import lax
from jax.experimental import pallas as pl
from jax.experimental.pallas import tpu as pltpu

D_MODEL = 1024
HEAD_DIM = 64
CONV_CH = 256
CONV_WIDTH = 31
ATTN_HEADS = 8
ATTN_CH = ATTN_HEADS * HEAD_DIM
DILATED_PATTERNS = ((128, 1), (512, 4), (2048, 16))
ATTN_BLOCK = 128
N_BUCKETS = 32
MAX_DISTANCE = 2048
GMLP_CH = 256
GMLP_GROUPS = 4
GMLP_GROUP_DIM = GMLP_CH // GMLP_GROUPS
CHUNK = 128
MIX_CH = CONV_CH + ATTN_CH + GMLP_CH
IN_CH = 2 * CONV_CH + 3 * ATTN_CH + 2 * GMLP_CH
D_FF = 2816
FFN_CONV_WIDTH = 3
LN_EPS = 1e-5

LANES = 128
COL_BLOCK = 512
COL_A, COL_Q, COL_K, COL_V, COL_C = range(IN_CH // COL_BLOCK)
QKV_Q, QKV_K, QKV_V = range(3)

ROW_TILE = 512
CONV_HALO = 32
CONV_ROW_CHUNK = 64
HEAD_PAIR = 2 * HEAD_DIM
N_PAIRS = ATTN_HEADS // 2
ATTN_TILE = ATTN_BLOCK * max(d for _, d in DILATED_PATTERNS)
ATTN_AHEAD = 1
ATTN_GROUP = 1
FF_CHUNK = 256
FFN_LOOKAHEAD = 1
FFN_SLOTS = FFN_LOOKAHEAD + 1
FFN_ROWS = 64
POST_TILES = 1
FFN_HALO = 8
LOG2_E = math.log2(math.e)
NEG_LOGIT = -1e30
VMEM_LIMIT_BYTES = 56 * 1024 * 1024

BF16 = jnp.bfloat16
F32 = jnp.float32


def _const_spec(shape):
    zeros = (0,) * len(shape)
    return pl.BlockSpec(shape, lambda *_: zeros, pipeline_mode=pl.Buffered(1))


def _const_operand(c):
    if not isinstance(c, tuple):
        return c, _const_spec(c.shape)
    stacked, layer = c
    index = (layer,) + (0,) * (stacked.ndim - 1)
    return stacked, pl.BlockSpec((None,) + stacked.shape[1:], lambda *_: index,
                                 pipeline_mode=pl.Buffered(1))


def _params(*semantics):
    return pltpu.CompilerParams(dimension_semantics=semantics,
                                vmem_limit_bytes=VMEM_LIMIT_BYTES)


def _layernorm(x, g, b):
    mu = jnp.mean(x, axis=-1, keepdims=True)
    xc = x - mu
    var = jnp.mean(xc * xc, axis=-1, keepdims=True)
    return xc * lax.rsqrt(var + LN_EPS) * g + b


def _sigmoid(x):
    return 0.5 * jnp.tanh(0.5 * x) + 0.5


def _inproj_body(tiles_per_seq, x_ref, w_ref, b_ref, dww_ref, dwb_ref, cg_ref, cb_ref,
                 gg_ref, gb_ref, ws_ref, bs_ref, qkv_ref, o_ref, xb_ref, buf_ref, c_ref):
    first = (pl.program_id(0) % tiles_per_seq) == 0
    xb_ref[...] = x_ref[...].astype(BF16)

    @pl.when(pl.program_id(0) == 0)
    def _():
        buf_ref[...] = jnp.zeros_like(buf_ref)

    def project(col):
        cols = slice(col * COL_BLOCK, (col + 1) * COL_BLOCK)
        return jnp.dot(xb_ref[...], w_ref[:, cols], preferred_element_type=F32) + b_ref[:, cols]

    a = project(COL_A)
    glu = a[:, :CONV_CH] * _sigmoid(a[:, CONV_CH:])
    slabs = CONV_CH // LANES
    for s in range(slabs):
        lanes = slice(s * LANES, (s + 1) * LANES)
        prev_tail = buf_ref[s, pl.ds(2 * ROW_TILE, CONV_HALO, stride=2), :]
        buf_ref[s, pl.ds(0, CONV_HALO, stride=2), :] = jnp.where(first, 0.0, prev_tail)
        buf_ref[s, pl.ds(2 * CONV_HALO, ROW_TILE, stride=2), :] = glu[:, lanes]
    lead = CONV_HALO - (CONV_WIDTH - 1)

    def conv_rows(r0):
        accs = []
        for s in range(slabs):
            lanes = slice(s * LANES, (s + 1) * LANES)
            acc = jnp.broadcast_to(dwb_ref[:, lanes], (CONV_ROW_CHUNK, LANES))
            for k in range(CONV_WIDTH):
                window = pl.ds(2 * (r0 + lead + k), CONV_ROW_CHUNK, stride=2)
                acc = acc + dww_ref[k:k + 1, lanes] * buf_ref[s, window, :]
            accs.append(acc)
        y = _layernorm(jnp.concatenate(accs, axis=1), cg_ref[...], cb_ref[...])
        o_ref[r0:r0 + CONV_ROW_CHUNK, 0:CONV_CH] = (y * _sigmoid(y)).astype(BF16)

    c_ref[...] = project(COL_C)
    for j, col in enumerate((COL_Q, COL_K, COL_V)):
        qkv_ref[:, j * COL_BLOCK:(j + 1) * COL_BLOCK] = project(col)
    for r0 in range(0, ROW_TILE, CONV_ROW_CHUNK):
        conv_rows(r0)

    lane = lax.broadcasted_iota(jnp.int32, (1, GMLP_CH), 1)
    for ch in range(ROW_TILE // CHUNK):
        rows = slice(ch * CHUNK, (ch + 1) * CHUNK)
        c = c_ref[rows, :]
        vn = _layernorm(c[:, GMLP_CH:], gg_ref[...], gb_ref[...])
        stacked = jnp.concatenate(
            [jnp.where((lane // GMLP_GROUP_DIM) == g, vn, 0.0) for g in range(GMLP_GROUPS)],
            axis=0).astype(BF16)
        mixed = jnp.dot(ws_ref[...], stacked, preferred_element_type=F32) + bs_ref[...]
        o_ref[rows, CONV_CH:] = (c[:, :GMLP_CH] * mixed).astype(BF16)


def _inproj(x, seq, w, b, dww, dwb, cg, cb, gg, gb, ws_cat, bs_tab):
    n = x.shape[0]
    consts, const_specs = zip(*map(_const_operand, [w, b, dww, dwb, cg, cb, gg, gb, ws_cat, bs_tab]))
    row = lambda width: pl.BlockSpec((ROW_TILE, width), lambda i: (i, 0))
    return pl.pallas_call(
        functools.partial(_inproj_body, seq // ROW_TILE),
        grid=(n // ROW_TILE,),
        in_specs=[row(D_MODEL)] + list(const_specs),
        out_specs=[row(3 * ATTN_CH), row(CONV_CH + GMLP_CH)],
        out_shape=[jax.ShapeDtypeStruct((n, 3 * ATTN_CH), F32),
                   jax.ShapeDtypeStruct((n, CONV_CH + GMLP_CH), BF16)],
        scratch_shapes=[pltpu.VMEM((ROW_TILE, D_MODEL), BF16),
                        pltpu.VMEM((CONV_CH // LANES, 2 * (CONV_HALO + ROW_TILE), LANES), F32),
                        pltpu.VMEM((ROW_TILE, 2 * GMLP_CH), F32)],
        compiler_params=_params("arbitrary"),
        name="inproj_conv_gmlp",
    )(x, *consts)


def _attn_scores(q2, k2, bias_lo, bias_hi):
    lo = lax.broadcasted_iota(jnp.int32, (1, HEAD_PAIR), 1) < HEAD_DIM
    nt = (((1,), (1,)), ((), ()))
    q2 = q2 * (HEAD_DIM ** -0.5 * LOG2_E)
    kb = k2.astype(BF16)
    ps, ms = [], []
    for keep, bias in ((lo, bias_lo), (jnp.logical_not(lo), bias_hi)):
        qh = jnp.where(keep, q2, 0.0).astype(BF16)
        s = lax.dot_general(qh, kb, nt, preferred_element_type=F32) + bias
        m = jnp.max(s, axis=-1, keepdims=True)
        ps.append(jnp.exp2(s - m).astype(BF16))
        ms.append(m)
    return jnp.concatenate(ps, axis=1), jnp.where(lo, ms[0], ms[1])


def _attn_values(p, v2):
    lo = lax.broadcasted_iota(jnp.int32, (1, HEAD_PAIR), 1) < HEAD_DIM
    shape = (2 * ATTN_BLOCK, HEAD_PAIR)
    ind_lo = jnp.broadcast_to(jnp.where(lo, 1.0, 0.0), shape).astype(BF16)
    ind_hi = jnp.broadcast_to(jnp.where(lo, 0.0, 1.0), shape).astype(BF16)
    vb = v2.astype(BF16)
    zero = jnp.zeros_like(vb)
    v_stack = jnp.concatenate(
        [jnp.concatenate([jnp.where(lo, vb, zero), ind_lo], axis=1),
         jnp.concatenate([jnp.where(lo, zero, vb), ind_hi], axis=1)], axis=0)
    ol = jnp.dot(p, v_stack, preferred_element_type=F32)
    return ol[:, :HEAD_PAIR], ol[:, HEAD_PAIR:]


def _attn_body(q_ref, kc_ref, kp_ref, vc_ref, vp_ref, bias_ref, out_ref,
               p_scr, o_scr, m_scr, l_scr):
    pair = pl.program_id(1)
    edge = jnp.where(pl.program_id(2) == 0, 0, 1)

    def rows(start, n, d):
        return pl.ds(start, n) if d == 1 else pl.ds(start, n, stride=d)

    def window(cur_ref, prev_ref, d, start, span, seq_edge):
        if seq_edge:
            return jnp.concatenate(
                [prev_ref[rows(ATTN_TILE + start - span, ATTN_BLOCK, d), :],
                 cur_ref[rows(start, ATTN_BLOCK, d), :]], axis=0)
        return cur_ref[rows(start - span, 2 * ATTN_BLOCK, d), :]

    order = sorted(range(len(DILATED_PATTERNS)), key=lambda p: -DILATED_PATTERNS[p][1])
    assert DILATED_PATTERNS[order[-1]][1] == 1
    units = []
    for p in order:
        d = DILATED_PATTERNS[p][1]
        span = d * ATTN_BLOCK
        for r in range(d):
            for j in range(ATTN_TILE // span):
                units.append((p, d, r + j * span, span, j == 0))
    groups = [units[g:g + ATTN_GROUP] for g in range(0, len(units), ATTN_GROUP)]

    def merge(start):
        r = slice(start, start + ATTN_BLOCK)
        ms = [m_scr[p, r, :] for p in range(len(DILATED_PATTERNS))]
        big = functools.reduce(jnp.maximum, ms)
        ws = [jnp.exp2(m - big) for m in ms]
        num = sum(w * o_scr[p, r, :] for p, w in enumerate(ws))
        den = sum(w * l_scr[p, r, :] for p, w in enumerate(ws))
        out_ref[r, :] = (num / den).astype(BF16)

    def scores(group, slot):
        for t, (p, d, start, span, seq_edge) in enumerate(group):
            sel = edge if seq_edge else 1
            probs, m = _attn_scores(q_ref[rows(start, ATTN_BLOCK, d), :],
                                    window(kc_ref, kp_ref, d, start, span, seq_edge),
                                    bias_ref[p, sel, 2 * pair], bias_ref[p, sel, 2 * pair + 1])
            p_scr[slot, t] = probs
            m_scr[p, rows(start, ATTN_BLOCK, d), :] = m

    def values(group, slot):
        for t, (p, d, start, span, seq_edge) in enumerate(group):
            o, l = _attn_values(p_scr[slot, t], window(vc_ref, vp_ref, d, start, span, seq_edge))
            o_scr[p, rows(start, ATTN_BLOCK, d), :] = o
            l_scr[p, rows(start, ATTN_BLOCK, d), :] = l
            if d == 1:
                merge(start)

    slots = ATTN_AHEAD + 1
    for g in range(ATTN_AHEAD):
        scores(groups[g], g % slots)
    for g in range(len(groups)):
        if g + ATTN_AHEAD < len(groups):
            scores(groups[g + ATTN_AHEAD], (g + ATTN_AHEAD) % slots)
        values(groups[g], g % slots)


def _attention(h, batch, seq, bias):
    n = h.shape[0]
    tiles = seq // ATTN_TILE
    lane_blocks = COL_BLOCK // LANES

    def cur(col):
        return pl.BlockSpec((ATTN_TILE, LANES),
                            lambda b, hp, t: (b * tiles + t, col * lane_blocks + hp))

    def prev(col):
        return pl.BlockSpec((ATTN_TILE, LANES),
                            lambda b, hp, t: (b * tiles + jnp.maximum(t - 1, 0),
                                              col * lane_blocks + hp))

    return pl.pallas_call(
        _attn_body,
        grid=(batch, N_PAIRS, tiles),
        in_specs=[cur(QKV_Q), cur(QKV_K), prev(QKV_K), cur(QKV_V), prev(QKV_V),
                  _const_spec(bias.shape)],
        out_specs=pl.BlockSpec((ATTN_TILE, LANES), lambda b, hp, t: (b * tiles + t, hp)),
        out_shape=jax.ShapeDtypeStruct((n, ATTN_CH), BF16),
        scratch_shapes=[pltpu.VMEM((ATTN_AHEAD + 1, ATTN_GROUP, ATTN_BLOCK, 4 * ATTN_BLOCK), BF16),
                        pltpu.VMEM((len(DILATED_PATTERNS), ATTN_TILE, LANES), F32),
                        pltpu.VMEM((len(DILATED_PATTERNS), ATTN_TILE, LANES), F32),
                        pltpu.VMEM((len(DILATED_PATTERNS), ATTN_TILE, LANES), F32)],
        compiler_params=_params("parallel", "parallel", "arbitrary"),
        name="dilated_attn",
    )(h, h, h, h, h, bias)


def _t5_bucket(dist):
    max_exact = N_BUCKETS // 2
    d = np.maximum(dist, 1).astype(np.float64)
    large = max_exact + (np.log(d / max_exact) / math.log(MAX_DISTANCE / max_exact)
                         * (N_BUCKETS - max_exact)).astype(np.int32)
    large = np.minimum(large, N_BUCKETS - 1)
    return np.where(dist < max_exact, dist, large).astype(np.int32)


def _attn_bias(rel_table, window, dilation):
    n_win = window // dilation
    assert n_win <= ATTN_BLOCK
    heads = rel_table.shape[1]
    onehot = np.eye(N_BUCKETS, dtype=np.float32)[_t5_bucket(np.arange(n_win + 1) * dilation)]
    by_dist = jnp.sum(onehot[:, :, None] * rel_table.astype(F32)[None], axis=1)
    period = 3 * ATTN_BLOCK + 1
    u = jnp.concatenate([jnp.full((heads, ATTN_BLOCK - n_win), NEG_LOGIT, F32), by_dist[::-1].T,
                         jnp.full((heads, period - ATTN_BLOCK - 1), NEG_LOGIT, F32)], axis=1)
    toeplitz = jnp.tile(u, (1, ATTN_BLOCK))[:, :ATTN_BLOCK * (period - 1)]
    main = toeplitz.reshape(heads, ATTN_BLOCK, period - 1)[:, :, :2 * ATTN_BLOCK]
    kj = np.arange(2 * ATTN_BLOCK)[None, None, :]
    first = jnp.where(jnp.asarray(kj >= ATTN_BLOCK), main, NEG_LOGIT)
    return jnp.stack([first, main]) * LOG2_E


def _post_body(alpha, tiles_per_seq, ac_ref, attn_ref, x_ref, wo_ref, bo_ref, g1_ref, b1_ref,
               wup_ref, bup_ref, cw_ref, cb_ref, wdn_ref, bdn_ref, g2_ref, b2_ref, y_ref,
               x1_ref, xb_ref, h_buf, act_ref, tail_ref):
    mix_half = CONV_CH + GMLP_CH
    for t in range(POST_TILES):
        rows = slice(t * ROW_TILE, (t + 1) * ROW_TILE)
        mix = (jnp.dot(ac_ref[rows, :], wo_ref[0:mix_half, :], preferred_element_type=F32)
               + jnp.dot(attn_ref[rows, :], wo_ref[mix_half:, :], preferred_element_type=F32)
               + bo_ref[...])
        x1 = _layernorm(alpha * x_ref[rows, :] + mix, g1_ref[...], b1_ref[...])
        x1_ref[t] = x1
        xb_ref[t] = x1.astype(BF16)

    tile0 = pl.program_id(0) * POST_TILES
    firsts = [((tile0 + t) % tiles_per_seq) == 0 for t in range(POST_TILES)]
    half_slabs = FF_CHUNK // LANES
    slabs = 2 * half_slabs
    n_chunks = D_FF // FF_CHUNK

    def chunk_cols(c, half):
        start = half * D_FF + c * FF_CHUNK
        return slice(start, start + FF_CHUNK)

    steps = [(t, c) for t in range(POST_TILES) for c in range(n_chunks)]

    def up(n):
        t, c = steps[n]
        for half in range(2):
            cols = chunk_cols(c, half)
            h = jnp.dot(xb_ref[t], wup_ref[:, cols], preferred_element_type=F32)
            halo = jnp.where(firsts[t], -bup_ref[:, cols], tail_ref[:, cols])
            tail_ref[:, cols] = h[ROW_TILE - FFN_HALO:, :]
            for s in range(half_slabs):
                lanes = slice(s * LANES, (s + 1) * LANES)
                buf = h_buf.at[n % FFN_SLOTS, half * half_slabs + s]
                buf[pl.ds(0, FFN_HALO, stride=2), :] = halo[:, lanes]
                buf[pl.ds(2 * FFN_HALO, ROW_TILE, stride=2), :] = h[:, lanes]

    def conv(n, s, r0):
        start = chunk_cols(steps[n][1], s // half_slabs).start + (s % half_slabs) * LANES
        wcols = slice(start, start + LANES)
        y = cb_ref[:, wcols]
        for k in range(FFN_CONV_WIDTH):
            shift = FFN_CONV_WIDTH - 1 - k
            y = y + cw_ref[k:k + 1, wcols] * h_buf[n % FFN_SLOTS, s, pl.ds(
                2 * (FFN_HALO + r0 - shift), FFN_ROWS, stride=2), :]
        return y

    for n in range(FFN_LOOKAHEAD):
        up(n)
    acc = None
    for n, (t, c) in enumerate(steps):
        if n + FFN_LOOKAHEAD < len(steps):
            up(n + FFN_LOOKAHEAD)
        for r0 in range(0, ROW_TILE, FFN_ROWS):
            for s in range(half_slabs):
                gate, val = conv(n, s, r0), conv(n, s + half_slabs, r0)
                act_ref[n % FFN_SLOTS, r0:r0 + FFN_ROWS, s * LANES:(s + 1) * LANES] = (
                    gate * _sigmoid(gate) * val).astype(BF16)
        if c == 0:
            acc = alpha * x1_ref[t] + bdn_ref[...]
        acc = acc + jnp.dot(act_ref[n % FFN_SLOTS], wdn_ref[c * FF_CHUNK:(c + 1) * FF_CHUNK, :],
                            preferred_element_type=F32)
        if c == n_chunks - 1:
            y_ref[t * ROW_TILE:(t + 1) * ROW_TILE, :] = _layernorm(acc, g2_ref[...], b2_ref[...])


def _post(alpha, ac, attn, x, seq, wo, bo, g1, b1, wup, bup, cw, cb, wdn, bdn, g2, b2):
    n = x.shape[0]
    rows = POST_TILES * ROW_TILE
    row = lambda width: pl.BlockSpec((rows, width), lambda i: (i, 0))
    consts, const_specs = zip(*map(_const_operand,
                                   [wo, bo, g1, b1, wup, bup, cw, cb, wdn, bdn, g2, b2]))
    return pl.pallas_call(
        functools.partial(_post_body, alpha, seq // ROW_TILE),
        grid=(n // rows,),
        in_specs=[row(CONV_CH + GMLP_CH), row(ATTN_CH), row(D_MODEL)] + list(const_specs),
        out_specs=row(D_MODEL),
        out_shape=jax.ShapeDtypeStruct((n, D_MODEL), F32),
        scratch_shapes=[pltpu.VMEM((POST_TILES, ROW_TILE, D_MODEL), F32),
                        pltpu.VMEM((POST_TILES, ROW_TILE, D_MODEL), BF16),
                        pltpu.VMEM((FFN_SLOTS, 2 * FF_CHUNK // LANES, 2 * (FFN_HALO + ROW_TILE),
                                    LANES), F32),
                        pltpu.VMEM((FFN_SLOTS, ROW_TILE, FF_CHUNK), BF16),
                        pltpu.VMEM((FFN_HALO, 2 * D_FF), F32)],
        compiler_params=_params("arbitrary"),
        name="outproj_convffn",
    )(ac, attn, x, *consts)


def kernel(x, w_in, b_in, conv_dw_w, conv_dw_b, conv_ln_g, conv_ln_b, rel_bias_table, gmlp_ln_g, gmlp_ln_b, gmlp_w_s, gmlp_b_s, w_out, b_out, ln1_g, ln1_b, ffn_w_up, ffn_b_up, ffn_conv_w, ffn_conv_b, ffn_w_down, ffn_b_down, ln2_g, ln2_b):
    batch, seq, _ = x.shape
    depth = w_in.shape[0]
    alpha = (2.0 * depth) ** 0.25
    n = batch * seq
    row2d = lambda v: v.reshape(1, -1)
    bias = jnp.stack([_attn_bias(rel_bias_table, w, d) for (w, d) in DILATED_PATTERNS])
    w_in_b, w_up_b, w_down_b = (w.astype(BF16) for w in (w_in, ffn_w_up, ffn_w_down))
    w_out_b = w_out.astype(BF16)
    w_out_b = jnp.concatenate([w_out_b[:, :CONV_CH], w_out_b[:, CONV_CH + ATTN_CH:],
                               w_out_b[:, CONV_CH:CONV_CH + ATTN_CH]], axis=1)

    xf = x.reshape(n, D_MODEL)
    for l in range(depth):
        ws_cat = jnp.transpose(jnp.tril(gmlp_w_s[l]), (1, 0, 2)).reshape(CHUNK, GMLP_GROUPS * CHUNK)
        bs_tab = jnp.repeat(gmlp_b_s[l].T, GMLP_GROUP_DIM, axis=1)
        qkv, ac = _inproj(xf, seq, (w_in_b, l), row2d(b_in[l]), conv_dw_w[l],
                          row2d(conv_dw_b[l]), row2d(conv_ln_g[l]), row2d(conv_ln_b[l]),
                          row2d(gmlp_ln_g[l]), row2d(gmlp_ln_b[l]), ws_cat.astype(BF16), bs_tab)
        attn = _attention(qkv, batch, seq, bias)
        xf = _post(alpha, ac, attn, xf, seq, (w_out_b, l), row2d(b_out[l]), row2d(ln1_g[l]),
                   row2d(ln1_b[l]), (w_up_b, l), row2d(ffn_b_up[l]), (ffn_conv_w, l),
                   row2d(ffn_conv_b[l] + ffn_b_up[l] * jnp.sum(ffn_conv_w[l], axis=0)),
                   (w_down_b, l), row2d(ffn_b_down[l]), row2d(ln2_g[l]), row2d(ln2_b[l]))
    return xf.reshape(batch, seq, D_MODEL)
```

```python
import functools
import math

import jax
import jax.numpy as jnp
import numpy as np
from jax import lax
from jax.experimental import pallas as pl
from jax.experimental.pallas import tpu as pltpu

D_MODEL = 1024
HEAD_DIM = 64
CONV_CH = 256
CONV_WIDTH = 31
ATTN_HEADS = 8
ATTN_CH = ATTN_HEADS * HEAD_DIM
DILATED_PATTERNS = ((128, 1), (512, 4), (2048, 16))
ATTN_BLOCK = 128
N_BUCKETS = 32
MAX_DISTANCE = 2048
GMLP_CH = 256
GMLP_GROUPS = 4
GMLP_GROUP_DIM = GMLP_CH // GMLP_GROUPS
CHUNK = 128
MIX_CH = CONV_CH + ATTN_CH + GMLP_CH
IN_CH = 2 * CONV_CH + 3 * ATTN_CH + 2 * GMLP_CH
D_FF = 2816
FFN_CONV_WIDTH = 3
LN_EPS = 1e-5

LANES = 128
COL_BLOCK = 512
COL_A, COL_Q, COL_K, COL_V, COL_C = range(IN_CH // COL_BLOCK)
QKV_Q, QKV_K, QKV_V = range(3)

ROW_TILE = 512
CONV_HALO = 32
CONV_ROW_CHUNK = 64
HEAD_PAIR = 2 * HEAD_DIM
N_PAIRS = ATTN_HEADS // 2
ATTN_TILE = ATTN_BLOCK * max(d for _, d in DILATED_PATTERNS)
ATTN_AHEAD = 1
ATTN_GROUP = 1
FF_CHUNK = 256
FFN_LOOKAHEAD = 1
FFN_SLOTS = FFN_LOOKAHEAD + 1
FFN_ROWS = 64
POST_TILES = 1
FFN_HALO = 8
LOG2_E = math.log2(math.e)
NEG_LOGIT = -1e30
V7X_VMEM_BYTES = 64 * 1024 * 1024
VMEM_LIMIT_BYTES = V7X_VMEM_BYTES - 8 * 1024 * 1024

BF16 = jnp.bfloat16
F32 = jnp.float32


def _const_spec(shape):
    zeros = (0,) * len(shape)
    return pl.BlockSpec(shape, lambda *_: zeros, pipeline_mode=pl.Buffered(1))


def _const_operand(c):
    if not isinstance(c, tuple):
        return c, _const_spec(c.shape)
    stacked, layer = c
    index = (layer,) + (0,) * (stacked.ndim - 1)
    return stacked, pl.BlockSpec((None,) + stacked.shape[1:], lambda *_: index,
                                 pipeline_mode=pl.Buffered(1))


def _params(*semantics):
    return pltpu.CompilerParams(dimension_semantics=semantics,
                                vmem_limit_bytes=VMEM_LIMIT_BYTES)


def _layernorm(x, g, b):
    mu = jnp.mean(x, axis=-1, keepdims=True)
    xc = x - mu
    var = jnp.mean(xc * xc, axis=-1, keepdims=True)
    return xc * lax.rsqrt(var + LN_EPS) * g + b


def _sigmoid(x):
    return 0.5 * jnp.tanh(0.5 * x) + 0.5


def _inproj_body(tiles_per_seq, x_ref, w_ref, b_ref, dww_ref, dwb_ref, cg_ref, cb_ref,
                 gg_ref, gb_ref, ws_ref, bs_ref, qkv_ref, o_ref, xb_ref, buf_ref, c_ref):
    first = (pl.program_id(0) % tiles_per_seq) == 0
    xb_ref[...] = x_ref[...].astype(BF16)

    @pl.when(pl.program_id(0) == 0)
    def _():
        buf_ref[...] = jnp.zeros_like(buf_ref)

    def project(col):
        cols = slice(col * COL_BLOCK, (col + 1) * COL_BLOCK)
        return jnp.dot(xb_ref[...], w_ref[:, cols], preferred_element_type=F32) + b_ref[:, cols]

    a = project(COL_A)
    glu = a[:, :CONV_CH] * _sigmoid(a[:, CONV_CH:])
    slabs = CONV_CH // LANES
    for s in range(slabs):
        lanes = slice(s * LANES, (s + 1) * LANES)
        prev_tail = buf_ref[s, pl.ds(2 * ROW_TILE, CONV_HALO, stride=2), :]
        buf_ref[s, pl.ds(0, CONV_HALO, stride=2), :] = jnp.where(first, 0.0, prev_tail)
        buf_ref[s, pl.ds(2 * CONV_HALO, ROW_TILE, stride=2), :] = glu[:, lanes]
    lead = CONV_HALO - (CONV_WIDTH - 1)

    def conv_rows(r0):
        accs = []
        for s in range(slabs):
            lanes = slice(s * LANES, (s + 1) * LANES)
            acc = jnp.broadcast_to(dwb_ref[:, lanes], (CONV_ROW_CHUNK, LANES))
            for k in range(CONV_WIDTH):
                window = pl.ds(2 * (r0 + lead + k), CONV_ROW_CHUNK, stride=2)
                acc = acc + dww_ref[k:k + 1, lanes] * buf_ref[s, window, :]
            accs.append(acc)
        y = _layernorm(jnp.concatenate(accs, axis=1), cg_ref[...], cb_ref[...])
        o_ref[r0:r0 + CONV_ROW_CHUNK, 0:CONV_CH] = (y * _sigmoid(y)).astype(BF16)

    c_ref[...] = project(COL_C)
    for j, col in enumerate((COL_Q, COL_K, COL_V)):
        qkv_ref[:, j * COL_BLOCK:(j + 1) * COL_BLOCK] = project(col)
    for r0 in range(0, ROW_TILE, CONV_ROW_CHUNK):
        conv_rows(r0)

    lane = lax.broadcasted_iota(jnp.int32, (1, GMLP_CH), 1)
    for ch in range(ROW_TILE // CHUNK):
        rows = slice(ch * CHUNK, (ch + 1) * CHUNK)
        c = c_ref[rows, :]
        vn = _layernorm(c[:, GMLP_CH:], gg_ref[...], gb_ref[...])
        stacked = jnp.concatenate(
            [jnp.where((lane // GMLP_GROUP_DIM) == g, vn, 0.0) for g in range(GMLP_GROUPS)],
            axis=0).astype(BF16)
        mixed = jnp.dot(ws_ref[...], stacked, preferred_element_type=F32) + bs_ref[...]
        o_ref[rows, CONV_CH:] = (c[:, :GMLP_CH] * mixed).astype(BF16)


def _inproj(x, seq, w, b, dww, dwb, cg, cb, gg, gb, ws_cat, bs_tab):
    n = x.shape[0]
    consts, const_specs = zip(*map(_const_operand, [w, b, dww, dwb, cg, cb, gg, gb, ws_cat, bs_tab]))
    row = lambda width: pl.BlockSpec((ROW_TILE, width), lambda i: (i, 0))
    return pl.pallas_call(
        functools.partial(_inproj_body, seq // ROW_TILE),
        grid=(n // ROW_TILE,),
        in_specs=[row(D_MODEL)] + list(const_specs),
        out_specs=[row(3 * ATTN_CH), row(CONV_CH + GMLP_CH)],
        out_shape=[jax.ShapeDtypeStruct((n, 3 * ATTN_CH), F32),
                   jax.ShapeDtypeStruct((n, CONV_CH + GMLP_CH), BF16)],
        scratch_shapes=[pltpu.VMEM((ROW_TILE, D_MODEL), BF16),
                        pltpu.VMEM((CONV_CH // LANES, 2 * (CONV_HALO + ROW_TILE), LANES), F32),
                        pltpu.VMEM((ROW_TILE, 2 * GMLP_CH), F32)],
        compiler_params=_params("arbitrary"),
        name="inproj_conv_gmlp",
    )(x, *consts)


def _attn_scores(q2, k2, bias_lo, bias_hi):
    lo = lax.broadcasted_iota(jnp.int32, (1, HEAD_PAIR), 1) < HEAD_DIM
    nt = (((1,), (1,)), ((), ()))
    q2 = q2 * (HEAD_DIM ** -0.5 * LOG2_E)
    kb = k2.astype(BF16)
    ps, ms = [], []
    for keep, bias in ((lo, bias_lo), (jnp.logical_not(lo), bias_hi)):
        qh = jnp.where(keep, q2, 0.0).astype(BF16)
        s = lax.dot_general(qh, kb, nt, preferred_element_type=F32) + bias
        m = jnp.max(s, axis=-1, keepdims=True)
        ps.append(jnp.exp2(s - m).astype(BF16))
        ms.append(m)
    return jnp.concatenate(ps, axis=1), jnp.where(lo, ms[0], ms[1])


def _attn_values(p, v2):
    lo = lax.broadcasted_iota(jnp.int32, (1, HEAD_PAIR), 1) < HEAD_DIM
    shape = (2 * ATTN_BLOCK, HEAD_PAIR)
    ind_lo = jnp.broadcast_to(jnp.where(lo, 1.0, 0.0), shape).astype(BF16)
    ind_hi = jnp.broadcast_to(jnp.where(lo, 0.0, 1.0), shape).astype(BF16)
    vb = v2.astype(BF16)
    zero = jnp.zeros_like(vb)
    v_stack = jnp.concatenate(
        [jnp.concatenate([jnp.where(lo, vb, zero), ind_lo], axis=1),
         jnp.concatenate([jnp.where(lo, zero, vb), ind_hi], axis=1)], axis=0)
    ol = jnp.dot(p, v_stack, preferred_element_type=F32)
    return ol[:, :HEAD_PAIR], ol[:, HEAD_PAIR:]


def _attn_body(q_ref, kc_ref, kp_ref, vc_ref, vp_ref, bias_ref, out_ref,
               p_scr, o_scr, m_scr, l_scr):
    pair = pl.program_id(1)
    edge = jnp.where(pl.program_id(2) == 0, 0, 1)

    def rows(start, n, d):
        return pl.ds(start, n) if d == 1 else pl.ds(start, n, stride=d)

    def window(cur_ref, prev_ref, d, start, span, seq_edge):
        if seq_edge:
            return jnp.concatenate(
                [prev_ref[rows(ATTN_TILE + start - span, ATTN_BLOCK, d), :],
                 cur_ref[rows(start, ATTN_BLOCK, d), :]], axis=0)
        return cur_ref[rows(start - span, 2 * ATTN_BLOCK, d), :]

    order = sorted(range(len(DILATED_PATTERNS)), key=lambda p: -DILATED_PATTERNS[p][1])
    assert DILATED_PATTERNS[order[-1]][1] == 1
    units = []
    for p in order:
        d = DILATED_PATTERNS[p][1]
        span = d * ATTN_BLOCK
        for r in range(d):
            for j in range(ATTN_TILE // span):
                units.append((p, d, r + j * span, span, j == 0))
    groups = [units[g:g + ATTN_GROUP] for g in range(0, len(units), ATTN_GROUP)]

    def merge(start):
        r = slice(start, start + ATTN_BLOCK)
        ms = [m_scr[p, r, :] for p in range(len(DILATED_PATTERNS))]
        big = functools.reduce(jnp.maximum, ms)
        ws = [jnp.exp2(m - big) for m in ms]
        num = sum(w * o_scr[p, r, :] for p, w in enumerate(ws))
        den = sum(w * l_scr[p, r, :] for p, w in enumerate(ws))
        out_ref[r, :] = (num / den).astype(BF16)

    def scores(group, slot):
        for t, (p, d, start, span, seq_edge) in enumerate(group):
            sel = edge if seq_edge else 1
            probs, m = _attn_scores(q_ref[rows(start, ATTN_BLOCK, d), :],
                                    window(kc_ref, kp_ref, d, start, span, seq_edge),
                                    bias_ref[p, sel, 2 * pair], bias_ref[p, sel, 2 * pair + 1])
            p_scr[slot, t] = probs
            m_scr[p, rows(start, ATTN_BLOCK, d), :] = m

    def values(group, slot):
        for t, (p, d, start, span, seq_edge) in enumerate(group):
            o, l = _attn_values(p_scr[slot, t], window(vc_ref, vp_ref, d, start, span, seq_edge))
            o_scr[p, rows(start, ATTN_BLOCK, d), :] = o
            l_scr[p, rows(start, ATTN_BLOCK, d), :] = l
            if d == 1:
                merge(start)

    slots = ATTN_AHEAD + 1
    for g in range(ATTN_AHEAD):
        scores(groups[g], g % slots)
    for g in range(len(groups)):
        if g + ATTN_AHEAD < len(groups):
            scores(groups[g + ATTN_AHEAD], (g + ATTN_AHEAD) % slots)
        values(groups[g], g % slots)


def _attention(qkv, batch, seq, bias):
    n = qkv.shape[0]
    tiles = seq // ATTN_TILE
    lane_blocks = COL_BLOCK // LANES

    def cur(col):
        return pl.BlockSpec((ATTN_TILE, LANES),
                            lambda b, hp, t: (b * tiles + t, col * lane_blocks + hp))

    def prev(col):
        return pl.BlockSpec((ATTN_TILE, LANES),
                            lambda b, hp, t: (b * tiles + jnp.maximum(t - 1, 0),
                                              col * lane_blocks + hp))

    return pl.pallas_call(
        _attn_body,
        grid=(batch, N_PAIRS, tiles),
        in_specs=[cur(QKV_Q), cur(QKV_K), prev(QKV_K), cur(QKV_V), prev(QKV_V),
                  _const_spec(bias.shape)],
        out_specs=pl.BlockSpec((ATTN_TILE, LANES), lambda b, hp, t: (b * tiles + t, hp)),
        out_shape=jax.ShapeDtypeStruct((n, ATTN_CH), BF16),
        scratch_shapes=[pltpu.VMEM((ATTN_AHEAD + 1, ATTN_GROUP, ATTN_BLOCK, 4 * ATTN_BLOCK), BF16),
                        pltpu.VMEM((len(DILATED_PATTERNS), ATTN_TILE, LANES), F32),
                        pltpu.VMEM((len(DILATED_PATTERNS), ATTN_TILE, LANES), F32),
                        pltpu.VMEM((len(DILATED_PATTERNS), ATTN_TILE, LANES), F32)],
        compiler_params=_params("parallel", "parallel", "arbitrary"),
        name="dilated_attn",
    )(qkv, qkv, qkv, qkv, qkv, bias)


def _t5_bucket(dist):
    max_exact = N_BUCKETS // 2
    d = np.maximum(dist, 1).astype(np.float64)
    large = max_exact + (np.log(d / max_exact) / math.log(MAX_DISTANCE / max_exact)
                         * (N_BUCKETS - max_exact)).astype(np.int32)
    large = np.minimum(large, N_BUCKETS - 1)
    return np.where(dist < max_exact, dist, large).astype(np.int32)


def _attn_bias(rel_table, window, dilation):
    n_win = window // dilation
    assert n_win <= ATTN_BLOCK
    heads = rel_table.shape[1]
    onehot = np.eye(N_BUCKETS, dtype=np.float32)[_t5_bucket(np.arange(n_win + 1) * dilation)]
    by_dist = jnp.sum(onehot[:, :, None] * rel_table.astype(F32)[None], axis=1)
    period = 3 * ATTN_BLOCK + 1
    u = jnp.concatenate([jnp.full((heads, ATTN_BLOCK - n_win), NEG_LOGIT, F32), by_dist[::-1].T,
                         jnp.full((heads, period - ATTN_BLOCK - 1), NEG_LOGIT, F32)], axis=1)
    toeplitz = jnp.tile(u, (1, ATTN_BLOCK))[:, :ATTN_BLOCK * (period - 1)]
    main = toeplitz.reshape(heads, ATTN_BLOCK, period - 1)[:, :, :2 * ATTN_BLOCK]
    kj = np.arange(2 * ATTN_BLOCK)[None, None, :]
    first = jnp.where(jnp.asarray(kj >= ATTN_BLOCK), main, NEG_LOGIT)
    return jnp.stack([first, main]) * LOG2_E


def _post_body(alpha, tiles_per_seq, ac_ref, attn_ref, x_ref, wo_ref, bo_ref, g1_ref, b1_ref,
               wup_ref, bup_ref, cw_ref, cb_ref, wdn_ref, bdn_ref, g2_ref, b2_ref, y_ref,
               x1_ref, xb_ref, h_buf, act_ref, tail_ref):
    mix_half = CONV_CH + GMLP_CH
    for t in range(POST_TILES):
        rows = slice(t * ROW_TILE, (t + 1) * ROW_TILE)
        mix = (jnp.dot(ac_ref[rows, :], wo_ref[0:mix_half, :], preferred_element_type=F32)
               + jnp.dot(attn_ref[rows, :], wo_ref[mix_half:, :], preferred_element_type=F32)
               + bo_ref[...])
        x1 = _layernorm(alpha * x_ref[rows, :] + mix, g1_ref[...], b1_ref[...])
        x1_ref[t] = x1
        xb_ref[t] = x1.astype(BF16)

    @pl.when(pl.program_id(0) == 0)
    def _():
        tail_ref[...] = jnp.zeros_like(tail_ref)

    tile0 = pl.program_id(0) * POST_TILES
    firsts = [((tile0 + t) % tiles_per_seq) == 0 for t in range(POST_TILES)]
    half_slabs = FF_CHUNK // LANES
    slabs = 2 * half_slabs
    n_chunks = D_FF // FF_CHUNK

    def chunk_cols(c, half):
        start = half * D_FF + c * FF_CHUNK
        return slice(start, start + FF_CHUNK)

    steps = [(t, c) for t in range(POST_TILES) for c in range(n_chunks)]

    def up(n):
        t, c = steps[n]
        for half in range(2):
            cols = chunk_cols(c, half)
            h = jnp.dot(xb_ref[t], wup_ref[:, cols], preferred_element_type=F32)
            halo = jnp.where(firsts[t], -bup_ref[:, cols], tail_ref[:, cols])
            tail_ref[:, cols] = h[ROW_TILE - FFN_HALO:, :]
            for s in range(half_slabs):
                lanes = slice(s * LANES, (s + 1) * LANES)
                buf = h_buf.at[n % FFN_SLOTS, half * half_slabs + s]
                buf[pl.ds(0, FFN_HALO, stride=2), :] = halo[:, lanes]
                buf[pl.ds(2 * FFN_HALO, ROW_TILE, stride=2), :] = h[:, lanes]

    def conv(n, s, r0):
        start = chunk_cols(steps[n][1], s // half_slabs).start + (s % half_slabs) * LANES
        wcols = slice(start, start + LANES)
        y = cb_ref[:, wcols]
        for k in range(FFN_CONV_WIDTH):
            shift = FFN_CONV_WIDTH - 1 - k
            y = y + cw_ref[k:k + 1, wcols] * h_buf[n % FFN_SLOTS, s, pl.ds(
                2 * (FFN_HALO + r0 - shift), FFN_ROWS, stride=2), :]
        return y

    for n in range(FFN_LOOKAHEAD):
        up(n)
    acc = None
    for n, (t, c) in enumerate(steps):
        if n + FFN_LOOKAHEAD < len(steps):
            up(n + FFN_LOOKAHEAD)
        for r0 in range(0, ROW_TILE, FFN_ROWS):
            for s in range(half_slabs):
                gate, val = conv(n, s, r0), conv(n, s + half_slabs, r0)
                act_ref[n % FFN_SLOTS, r0:r0 + FFN_ROWS, s * LANES:(s + 1) * LANES] = (
                    gate * _sigmoid(gate) * val).astype(BF16)
        if c == 0:
            acc = alpha * x1_ref[t] + bdn_ref[...]
        acc = acc + jnp.dot(act_ref[n % FFN_SLOTS], wdn_ref[c * FF_CHUNK:(c + 1) * FF_CHUNK, :],
                            preferred_element_type=F32)
        if c == n_chunks - 1:
            y_ref[t * ROW_TILE:(t + 1) * ROW_TILE, :] = _layernorm(acc, g2_ref[...], b2_ref[...])


def _post(alpha, ac, attn, x, seq, wo, bo, g1, b1, wup, bup, cw, cb, wdn, bdn, g2, b2):
    n = x.shape[0]
    rows = POST_TILES * ROW_TILE
    row = lambda width: pl.BlockSpec((rows, width), lambda i: (i, 0))
    consts, const_specs = zip(*map(_const_operand,
                                   [wo, bo, g1, b1, wup, bup, cw, cb, wdn, bdn, g2, b2]))
    return pl.pallas_call(
        functools.partial(_post_body, alpha, seq // ROW_TILE),
        grid=(n // rows,),
        in_specs=[row(CONV_CH + GMLP_CH), row(ATTN_CH), row(D_MODEL)] + list(const_specs),
        out_specs=row(D_MODEL),
        out_shape=jax.ShapeDtypeStruct((n, D_MODEL), F32),
        scratch_shapes=[pltpu.VMEM((POST_TILES, ROW_TILE, D_MODEL), F32),
                        pltpu.VMEM((POST_TILES, ROW_TILE, D_MODEL), BF16),
                        pltpu.VMEM((FFN_SLOTS, 2 * FF_CHUNK // LANES, 2 * (FFN_HALO + ROW_TILE),
                                    LANES), F32),
                        pltpu.VMEM((FFN_SLOTS, ROW_TILE, FF_CHUNK), BF16),
                        pltpu.VMEM((FFN_HALO, 2 * D_FF), F32)],
        compiler_params=_params("arbitrary"),
        name="outproj_convffn",
    )(ac, attn, x, *consts)


def kernel(x, w_in, b_in, conv_dw_w, conv_dw_b, conv_ln_g, conv_ln_b, rel_bias_table, gmlp_ln_g, gmlp_ln_b, gmlp_w_s, gmlp_b_s, w_out, b_out, ln1_g, ln1_b, ffn_w_up, ffn_b_up, ffn_conv_w, ffn_conv_b, ffn_w_down, ffn_b_down, ln2_g, ln2_b):
    batch, seq, d_model = x.shape
    depth = w_in.shape[0]
    assert x.dtype == F32 and d_model == D_MODEL
    assert w_in.shape == (depth, D_MODEL, IN_CH) and w_out.shape == (depth, MIX_CH, D_MODEL)
    assert ffn_w_up.shape == (depth, D_MODEL, 2 * D_FF) and ffn_w_down.shape == (depth, D_FF, D_MODEL)
    assert conv_dw_w.shape == (depth, CONV_WIDTH, CONV_CH) and CONV_WIDTH - 1 <= CONV_HALO
    assert ffn_conv_w.shape == (depth, FFN_CONV_WIDTH, 2 * D_FF) and FFN_CONV_WIDTH - 1 <= FFN_HALO
    assert gmlp_w_s.shape == (depth, GMLP_GROUPS, CHUNK, CHUNK)
    assert rel_bias_table.shape == (N_BUCKETS, ATTN_HEADS)
    assert seq % ATTN_TILE == 0 and seq % (POST_TILES * ROW_TILE) == 0 and ROW_TILE % CHUNK == 0
    assert D_FF % FF_CHUNK == 0 and FF_CHUNK % LANES == 0
    alpha = (2.0 * depth) ** 0.25
    n = batch * seq
    row2d = lambda v: v.reshape(1, -1)
    bias = jnp.stack([_attn_bias(rel_bias_table, w, d) for (w, d) in DILATED_PATTERNS])
    w_in_b, w_up_b, w_down_b = (w.astype(BF16) for w in (w_in, ffn_w_up, ffn_w_down))
    w_out_b = w_out.astype(BF16)
    w_out_b = jnp.concatenate([w_out_b[:, :CONV_CH], w_out_b[:, CONV_CH + ATTN_CH:],
                               w_out_b[:, CONV_CH:CONV_CH + ATTN_CH]], axis=1)

    xf = x.reshape(n, D_MODEL)
    for l in range(depth):
        ws_cat = jnp.transpose(jnp.tril(gmlp_w_s[l]), (1, 0, 2)).reshape(CHUNK, GMLP_GROUPS * CHUNK)
        bs_tab = jnp.repeat(gmlp_b_s[l].T, GMLP_GROUP_DIM, axis=1)
        qkv, ac = _inproj(xf, seq, (w_in_b, l), row2d(b_in[l]), conv_dw_w[l],
                          row2d(conv_dw_b[l]), row2d(conv_ln_g[l]), row2d(conv_ln_b[l]),
                          row2d(gmlp_ln_g[l]), row2d(gmlp_ln_b[l]), ws_cat.astype(BF16), bs_tab)
        attn = _attention(qkv, batch, seq, bias)
        xf = _post(alpha, ac, attn, xf, seq, (w_out_b, l), row2d(b_out[l]), row2d(ln1_g[l]),
                   row2d(ln1_b[l]), (w_up_b, l), row2d(ffn_b_up[l]), (ffn_conv_w, l),
                   row2d(ffn_conv_b[l] + ffn_b_up[l] * jnp.sum(ffn_conv_w[l], axis=0)),
                   (w_down_b, l), row2d(ffn_b_down[l]), row2d(ln2_g[l]), row2d(ln2_b[l]))
    return xf.reshape(batch, seq, D_MODEL)
```

```python
import functools
import math

import jax
import jax.numpy as jnp
import numpy as np
from jax import lax
from jax.experimental import pallas as pl
from jax.experimental.pallas import tpu as pltpu

D_MODEL = 1024
HEAD_DIM = 64
CONV_CH = 256
CONV_WIDTH = 31
ATTN_HEADS = 8
ATTN_CH = ATTN_HEADS * HEAD_DIM
DILATED_PATTERNS = ((128, 1), (512, 4), (2048, 16))
ATTN_BLOCK = 128
N_BUCKETS = 32
MAX_DISTANCE = 2048
GMLP_CH = 256
GMLP_GROUPS = 4
GMLP_GROUP_DIM = GMLP_CH // GMLP_GROUPS
CHUNK = 128
MIX_CH = CONV_CH + ATTN_CH + GMLP_CH
IN_CH = 2 * CONV_CH + 3 * ATTN_CH + 2 * GMLP_CH
D_FF = 2816
FFN_CONV_WIDTH = 3
LN_EPS = 1e-5

LANES = 128
COL_BLOCK = 512
COL_A, COL_Q, COL_K, COL_V, COL_C = range(IN_CH // COL_BLOCK)
QKV_Q, QKV_K, QKV_V = range(3)

ROW_TILE = 512
CONV_HALO = 32
CONV_ROW_CHUNK = 64
HEAD_PAIR = 2 * HEAD_DIM
N_PAIRS = ATTN_HEADS // 2
ATTN_TILE = ATTN_BLOCK * max(d for _, d in DILATED_PATTERNS)
ATTN_AHEAD = 1
ATTN_GROUP = 1
FF_CHUNK = 256
FFN_LOOKAHEAD = 1
FFN_SLOTS = FFN_LOOKAHEAD + 1
FFN_ROWS = 64
POST_TILES = 1
FFN_HALO = 8
LOG2_E = math.log2(math.e)
NEG_LOGIT = -1e30
V7X_VMEM_BYTES = 64 * 1024 * 1024
VMEM_LIMIT_BYTES = V7X_VMEM_BYTES - 8 * 1024 * 1024

BF16 = jnp.bfloat16
F32 = jnp.float32


def _const_spec(shape):
    zeros = (0,) * len(shape)
    return pl.BlockSpec(shape, lambda *_: zeros, pipeline_mode=pl.Buffered(1))


def _const_operand(c):
    if not isinstance(c, tuple):
        return c, _const_spec(c.shape)
    stacked, layer = c
    index = (layer,) + (0,) * (stacked.ndim - 1)
    return stacked, pl.BlockSpec((None,) + stacked.shape[1:], lambda *_: index,
                                 pipeline_mode=pl.Buffered(1))


def _params(*semantics):
    return pltpu.CompilerParams(dimension_semantics=semantics,
                                vmem_limit_bytes=VMEM_LIMIT_BYTES)


def _layernorm(x, g, b):
    mu = jnp.mean(x, axis=-1, keepdims=True)
    xc = x - mu
    var = jnp.mean(xc * xc, axis=-1, keepdims=True)
    return xc * lax.rsqrt(var + LN_EPS) * g + b


def _sigmoid(x):
    return 0.5 * jnp.tanh(0.5 * x) + 0.5


def _inproj_body(tiles_per_seq, x_ref, w_ref, b_ref, dww_ref, dwb_ref, cg_ref, cb_ref,
                 gg_ref, gb_ref, ws_ref, bs_ref, qkv_ref, o_ref, xb_ref, buf_ref, c_ref):
    first = (pl.program_id(0) % tiles_per_seq) == 0
    xb_ref[...] = x_ref[...].astype(BF16)

    def project(col):
        cols = slice(col * COL_BLOCK, (col + 1) * COL_BLOCK)
        return jnp.dot(xb_ref[...], w_ref[:, cols], preferred_element_type=F32) + b_ref[:, cols]

    a = project(COL_A)
    glu = a[:, :CONV_CH] * _sigmoid(a[:, CONV_CH:])
    slabs = CONV_CH // LANES
    for s in range(slabs):
        lanes = slice(s * LANES, (s + 1) * LANES)
        prev_tail = buf_ref[s, pl.ds(2 * ROW_TILE, CONV_HALO, stride=2), :]
        buf_ref[s, pl.ds(0, CONV_HALO, stride=2), :] = jnp.where(first, 0.0, prev_tail)
        buf_ref[s, pl.ds(2 * CONV_HALO, ROW_TILE, stride=2), :] = glu[:, lanes]
    lead = CONV_HALO - (CONV_WIDTH - 1)

    def conv_rows(r0):
        accs = []
        for s in range(slabs):
            lanes = slice(s * LANES, (s + 1) * LANES)
            acc = jnp.broadcast_to(dwb_ref[:, lanes], (CONV_ROW_CHUNK, LANES))
            for k in range(CONV_WIDTH):
                window = pl.ds(2 * (r0 + lead + k), CONV_ROW_CHUNK, stride=2)
                acc = acc + dww_ref[k:k + 1, lanes] * buf_ref[s, window, :]
            accs.append(acc)
        y = _layernorm(jnp.concatenate(accs, axis=1), cg_ref[...], cb_ref[...])
        o_ref[r0:r0 + CONV_ROW_CHUNK, 0:CONV_CH] = (y * _sigmoid(y)).astype(BF16)

    c_ref[...] = project(COL_C)
    for j, col in enumerate((COL_Q, COL_K, COL_V)):
        qkv_ref[:, j * COL_BLOCK:(j + 1) * COL_BLOCK] = project(col)
    for r0 in range(0, ROW_TILE, CONV_ROW_CHUNK):
        conv_rows(r0)

    lane = lax.broadcasted_iota(jnp.int32, (1, GMLP_CH), 1)
    for ch in range(ROW_TILE // CHUNK):
        rows = slice(ch * CHUNK, (ch + 1) * CHUNK)
        c = c_ref[rows, :]
        vn = _layernorm(c[:, GMLP_CH:], gg_ref[...], gb_ref[...])
        stacked = jnp.concatenate(
            [jnp.where((lane // GMLP_GROUP_DIM) == g, vn, 0.0) for g in range(GMLP_GROUPS)],
            axis=0).astype(BF16)
        mixed = jnp.dot(ws_ref[...], stacked, preferred_element_type=F32) + bs_ref[...]
        o_ref[rows, CONV_CH:] = (c[:, :GMLP_CH] * mixed).astype(BF16)


def _inproj(x, seq, w, b, dww, dwb, cg, cb, gg, gb, ws_cat, bs_tab):
    n = x.shape[0]
    consts, const_specs = zip(*map(_const_operand, [w, b, dww, dwb, cg, cb, gg, gb, ws_cat, bs_tab]))
    row = lambda width: pl.BlockSpec((ROW_TILE, width), lambda i: (i, 0))
    return pl.pallas_call(
        functools.partial(_inproj_body, seq // ROW_TILE),
        grid=(n // ROW_TILE,),
        in_specs=[row(D_MODEL)] + list(const_specs),
        out_specs=[row(3 * ATTN_CH), row(CONV_CH + GMLP_CH)],
        out_shape=[jax.ShapeDtypeStruct((n, 3 * ATTN_CH), F32),
                   jax.ShapeDtypeStruct((n, CONV_CH + GMLP_CH), BF16)],
        scratch_shapes=[pltpu.VMEM((ROW_TILE, D_MODEL), BF16),
                        pltpu.VMEM((CONV_CH // LANES, 2 * (CONV_HALO + ROW_TILE), LANES), F32),
                        pltpu.VMEM((ROW_TILE, 2 * GMLP_CH), F32)],
        compiler_params=_params("arbitrary"),
        name="inproj_conv_gmlp",
    )(x, *consts)


def _attn_scores(q2, k2, bias_lo, bias_hi):
    lo = lax.broadcasted_iota(jnp.int32, (1, HEAD_PAIR), 1) < HEAD_DIM
    nt = (((1,), (1,)), ((), ()))
    q2 = q2 * (HEAD_DIM ** -0.5 * LOG2_E)
    kb = k2.astype(BF16)
    ps, ms = [], []
    for keep, bias in ((lo, bias_lo), (jnp.logical_not(lo), bias_hi)):
        qh = jnp.where(keep, q2, 0.0).astype(BF16)
        s = lax.dot_general(qh, kb, nt, preferred_element_type=F32) + bias
        m = jnp.max(s, axis=-1, keepdims=True)
        ps.append(jnp.exp2(s - m).astype(BF16))
        ms.append(m)
    return jnp.concatenate(ps, axis=1), jnp.where(lo, ms[0], ms[1])


def _attn_values(p, v2):
    lo = lax.broadcasted_iota(jnp.int32, (1, HEAD_PAIR), 1) < HEAD_DIM
    shape = (2 * ATTN_BLOCK, HEAD_PAIR)
    ind_lo = jnp.broadcast_to(jnp.where(lo, 1.0, 0.0), shape).astype(BF16)
    ind_hi = jnp.broadcast_to(jnp.where(lo, 0.0, 1.0), shape).astype(BF16)
    vb = v2.astype(BF16)
    zero = jnp.zeros_like(vb)
    v_stack = jnp.concatenate(
        [jnp.concatenate([jnp.where(lo, vb, zero), ind_lo], axis=1),
         jnp.concatenate([jnp.where(lo, zero, vb), ind_hi], axis=1)], axis=0)
    ol = jnp.dot(p, v_stack, preferred_element_type=F32)
    return ol[:, :HEAD_PAIR], ol[:, HEAD_PAIR:]


def _attn_body(q_ref, kc_ref, kp_ref, vc_ref, vp_ref, bias_ref, out_ref,
               p_scr, o_scr, m_scr, l_scr):
    pair = pl.program_id(1)
    edge = jnp.where(pl.program_id(2) == 0, 0, 1)

    def rows(start, n, d):
        return pl.ds(start, n) if d == 1 else pl.ds(start, n, stride=d)

    def window(cur_ref, prev_ref, d, start, span, seq_edge):
        if seq_edge:
            return jnp.concatenate(
                [prev_ref[rows(ATTN_TILE + start - span, ATTN_BLOCK, d), :],
                 cur_ref[rows(start, ATTN_BLOCK, d), :]], axis=0)
        return cur_ref[rows(start - span, 2 * ATTN_BLOCK, d), :]

    order = sorted(range(len(DILATED_PATTERNS)), key=lambda p: -DILATED_PATTERNS[p][1])
    assert DILATED_PATTERNS[order[-1]][1] == 1
    units = []
    for p in order:
        d = DILATED_PATTERNS[p][1]
        span = d * ATTN_BLOCK
        for r in range(d):
            for j in range(ATTN_TILE // span):
                units.append((p, d, r + j * span, span, j == 0))
    groups = [units[g:g + ATTN_GROUP] for g in range(0, len(units), ATTN_GROUP)]

    def merge(start):
        r = slice(start, start + ATTN_BLOCK)
        ms = [m_scr[p, r, :] for p in range(len(DILATED_PATTERNS))]
        big = functools.reduce(jnp.maximum, ms)
        ws = [jnp.exp2(m - big) for m in ms]
        num = sum(w * o_scr[p, r, :] for p, w in enumerate(ws))
        den = sum(w * l_scr[p, r, :] for p, w in enumerate(ws))
        out_ref[r, :] = (num / den).astype(BF16)

    def scores(group, slot):
        for t, (p, d, start, span, seq_edge) in enumerate(group):
            sel = edge if seq_edge else 1
            probs, m = _attn_scores(q_ref[rows(start, ATTN_BLOCK, d), :],
                                    window(kc_ref, kp_ref, d, start, span, seq_edge),
                                    bias_ref[p, sel, 2 * pair], bias_ref[p, sel, 2 * pair + 1])
            p_scr[slot, t] = probs
            m_scr[p, rows(start, ATTN_BLOCK, d), :] = m

    def values(group, slot):
        for t, (p, d, start, span, seq_edge) in enumerate(group):
            o, l = _attn_values(p_scr[slot, t], window(vc_ref, vp_ref, d, start, span, seq_edge))
            o_scr[p, rows(start, ATTN_BLOCK, d), :] = o
            l_scr[p, rows(start, ATTN_BLOCK, d), :] = l
            if d == 1:
                merge(start)

    slots = ATTN_AHEAD + 1
    for g in range(ATTN_AHEAD):
        scores(groups[g], g % slots)
    for g in range(len(groups)):
        if g + ATTN_AHEAD < len(groups):
            scores(groups[g + ATTN_AHEAD], (g + ATTN_AHEAD) % slots)
        values(groups[g], g % slots)


def _attention(qkv, batch, seq, bias):
    n = qkv.shape[0]
    tiles = seq // ATTN_TILE
    lane_blocks = COL_BLOCK // LANES

    def cur(col):
        return pl.BlockSpec((ATTN_TILE, LANES),
                            lambda b, hp, t: (b * tiles + t, col * lane_blocks + hp))

    def prev(col):
        return pl.BlockSpec((ATTN_TILE, LANES),
                            lambda b, hp, t: (b * tiles + jnp.maximum(t - 1, 0),
                                              col * lane_blocks + hp))

    return pl.pallas_call(
        _attn_body,
        grid=(batch, N_PAIRS, tiles),
        in_specs=[cur(QKV_Q), cur(QKV_K), prev(QKV_K), cur(QKV_V), prev(QKV_V),
                  _const_spec(bias.shape)],
        out_specs=pl.BlockSpec((ATTN_TILE, LANES), lambda b, hp, t: (b * tiles + t, hp)),
        out_shape=jax.ShapeDtypeStruct((n, ATTN_CH), BF16),
        scratch_shapes=[pltpu.VMEM((ATTN_AHEAD + 1, ATTN_GROUP, ATTN_BLOCK, 4 * ATTN_BLOCK), BF16),
                        pltpu.VMEM((len(DILATED_PATTERNS), ATTN_TILE, LANES), F32),
                        pltpu.VMEM((len(DILATED_PATTERNS), ATTN_TILE, LANES), F32),
                        pltpu.VMEM((len(DILATED_PATTERNS), ATTN_TILE, LANES), F32)],
        compiler_params=_params("parallel", "parallel", "arbitrary"),
        name="dilated_attn",
    )(qkv, qkv, qkv, qkv, qkv, bias)


def _t5_bucket(dist):
    max_exact = N_BUCKETS // 2
    d = np.maximum(dist, 1).astype(np.float64)
    large = max_exact + (np.log(d / max_exact) / math.log(MAX_DISTANCE / max_exact)
                         * (N_BUCKETS - max_exact)).astype(np.int32)
    large = np.minimum(large, N_BUCKETS - 1)
    return np.where(dist < max_exact, dist, large).astype(np.int32)


def _attn_bias(rel_table, window, dilation):
    n_win = window // dilation
    assert n_win <= ATTN_BLOCK
    heads = rel_table.shape[1]
    onehot = np.eye(N_BUCKETS, dtype=np.float32)[_t5_bucket(np.arange(n_win + 1) * dilation)]
    by_dist = jnp.sum(onehot[:, :, None] * rel_table.astype(F32)[None], axis=1)
    period = 3 * ATTN_BLOCK + 1
    u = jnp.concatenate([jnp.full((heads, ATTN_BLOCK - n_win), NEG_LOGIT, F32), by_dist[::-1].T,
                         jnp.full((heads, period - ATTN_BLOCK - 1), NEG_LOGIT, F32)], axis=1)
    toeplitz = jnp.tile(u, (1, ATTN_BLOCK))[:, :ATTN_BLOCK * (period - 1)]
    main = toeplitz.reshape(heads, ATTN_BLOCK, period - 1)[:, :, :2 * ATTN_BLOCK]
    kj = np.arange(2 * ATTN_BLOCK)[None, None, :]
    first = jnp.where(jnp.asarray(kj >= ATTN_BLOCK), main, NEG_LOGIT)
    return jnp.stack([first, main]) * LOG2_E


def _post_body(alpha, tiles_per_seq, ac_ref, attn_ref, x_ref, wo_ref, bo_ref, g1_ref, b1_ref,
               wup_ref, bup_ref, cw_ref, cb_ref, wdn_ref, bdn_ref, g2_ref, b2_ref, y_ref,
               x1_ref, xb_ref, h_buf, act_ref, tail_ref):
    mix_half = CONV_CH + GMLP_CH
    for t in range(POST_TILES):
        rows = slice(t * ROW_TILE, (t + 1) * ROW_TILE)
        mix = (jnp.dot(ac_ref[rows, :], wo_ref[0:mix_half, :], preferred_element_type=F32)
               + jnp.dot(attn_ref[rows, :], wo_ref[mix_half:, :], preferred_element_type=F32)
               + bo_ref[...])
        x1 = _layernorm(alpha * x_ref[rows, :] + mix, g1_ref[...], b1_ref[...])
        x1_ref[t] = x1
        xb_ref[t] = x1.astype(BF16)

    tile0 = pl.program_id(0) * POST_TILES
    firsts = [((tile0 + t) % tiles_per_seq) == 0 for t in range(POST_TILES)]
    half_slabs = FF_CHUNK // LANES
    slabs = 2 * half_slabs
    n_chunks = D_FF // FF_CHUNK

    def chunk_cols(c, half):
        start = half * D_FF + c * FF_CHUNK
        return slice(start, start + FF_CHUNK)

    steps = [(t, c) for t in range(POST_TILES) for c in range(n_chunks)]

    def up(n):
        t, c = steps[n]
        for half in range(2):
            cols = chunk_cols(c, half)
            h = jnp.dot(xb_ref[t], wup_ref[:, cols], preferred_element_type=F32)
            halo = jnp.where(firsts[t], -bup_ref[:, cols], tail_ref[:, cols])
            tail_ref[:, cols] = h[ROW_TILE - FFN_HALO:, :]
            for s in range(half_slabs):
                lanes = slice(s * LANES, (s + 1) * LANES)
                buf = h_buf.at[n % FFN_SLOTS, half * half_slabs + s]
                buf[pl.ds(0, FFN_HALO, stride=2), :] = halo[:, lanes]
                buf[pl.ds(2 * FFN_HALO, ROW_TILE, stride=2), :] = h[:, lanes]

    def conv(n, s, r0):
        start = chunk_cols(steps[n][1], s // half_slabs).start + (s % half_slabs) * LANES
        wcols = slice(start, start + LANES)
        y = cb_ref[:, wcols]
        for k in range(FFN_CONV_WIDTH):
            shift = FFN_CONV_WIDTH - 1 - k
            y = y + cw_ref[k:k + 1, wcols] * h_buf[n % FFN_SLOTS, s, pl.ds(
                2 * (FFN_HALO + r0 - shift), FFN_ROWS, stride=2), :]
        return y

    for n in range(FFN_LOOKAHEAD):
        up(n)
    acc = None
    for n, (t, c) in enumerate(steps):
        if n + FFN_LOOKAHEAD < len(steps):
            up(n + FFN_LOOKAHEAD)
        for r0 in range(0, ROW_TILE, FFN_ROWS):
            for s in range(half_slabs):
                gate, val = conv(n, s, r0), conv(n, s + half_slabs, r0)
                act_ref[n % FFN_SLOTS, r0:r0 + FFN_ROWS, s * LANES:(s + 1) * LANES] = (
                    gate * _sigmoid(gate) * val).astype(BF16)
        if c == 0:
            acc = alpha * x1_ref[t] + bdn_ref[...]
        acc = acc + jnp.dot(act_ref[n % FFN_SLOTS], wdn_ref[c * FF_CHUNK:(c + 1) * FF_CHUNK, :],
                            preferred_element_type=F32)
        if c == n_chunks - 1:
            y_ref[t * ROW_TILE:(t + 1) * ROW_TILE, :] = _layernorm(acc, g2_ref[...], b2_ref[...])


def _post(alpha, ac, attn, x, seq, wo, bo, g1, b1, wup, bup, cw, cb, wdn, bdn, g2, b2):
    n = x.shape[0]
    rows = POST_TILES * ROW_TILE
    row = lambda width: pl.BlockSpec((rows, width), lambda i: (i, 0))
    consts, const_specs = zip(*map(_const_operand,
                                   [wo, bo, g1, b1, wup, bup, cw, cb, wdn, bdn, g2, b2]))
    return pl.pallas_call(
        functools.partial(_post_body, alpha, seq // ROW_TILE),
        grid=(n // rows,),
        in_specs=[row(CONV_CH + GMLP_CH), row(ATTN_CH), row(D_MODEL)] + list(const_specs),
        out_specs=row(D_MODEL),
        out_shape=jax.ShapeDtypeStruct((n, D_MODEL), F32),
        scratch_shapes=[pltpu.VMEM((POST_TILES, ROW_TILE, D_MODEL), F32),
                        pltpu.VMEM((POST_TILES, ROW_TILE, D_MODEL), BF16),
                        pltpu.VMEM((FFN_SLOTS, 2 * FF_CHUNK // LANES, 2 * (FFN_HALO + ROW_TILE),
                                    LANES), F32),
                        pltpu.VMEM((FFN_SLOTS, ROW_TILE, FF_CHUNK), BF16),
                        pltpu.VMEM((FFN_HALO, 2 * D_FF), F32)],
        compiler_params=_params("arbitrary"),
        name="outproj_convffn",
    )(ac, attn, x, *consts)


def kernel(x, w_in, b_in, conv_dw_w, conv_dw_b, conv_ln_g, conv_ln_b, rel_bias_table, gmlp_ln_g, gmlp_ln_b, gmlp_w_s, gmlp_b_s, w_out, b_out, ln1_g, ln1_b, ffn_w_up, ffn_b_up, ffn_conv_w, ffn_conv_b, ffn_w_down, ffn_b_down, ln2_g, ln2_b):
    batch, seq, d_model = x.shape
    depth = w_in.shape[0]
    assert x.dtype == F32 and d_model == D_MODEL
    assert w_in.shape == (depth, D_MODEL, IN_CH) and w_out.shape == (depth, MIX_CH, D_MODEL)
    assert ffn_w_up.shape == (depth, D_MODEL, 2 * D_FF) and ffn_w_down.shape == (depth, D_FF, D_MODEL)
    assert conv_dw_w.shape == (depth, CONV_WIDTH, CONV_CH) and CONV_WIDTH - 1 <= CONV_HALO
    assert ffn_conv_w.shape == (depth, FFN_CONV_WIDTH, 2 * D_FF) and FFN_CONV_WIDTH - 1 <= FFN_HALO
    assert gmlp_w_s.shape == (depth, GMLP_GROUPS, CHUNK, CHUNK)
    assert rel_bias_table.shape == (N_BUCKETS, ATTN_HEADS)
    assert seq % ATTN_TILE == 0 and seq % (POST_TILES * ROW_TILE) == 0 and ROW_TILE % CHUNK == 0
    assert D_FF % FF_CHUNK == 0 and FF_CHUNK % LANES == 0
    alpha = (2.0 * depth) ** 0.25
    n = batch * seq
    row2d = lambda v: v.reshape(1, -1)
    bias = jnp.stack([_attn_bias(rel_bias_table, w, d) for (w, d) in DILATED_PATTERNS])
    w_in_b, w_up_b, w_down_b = (w.astype(BF16) for w in (w_in, ffn_w_up, ffn_w_down))
    w_out_b = w_out.astype(BF16)
    w_out_b = jnp.concatenate([w_out_b[:, :CONV_CH], w_out_b[:, CONV_CH + ATTN_CH:],
                               w_out_b[:, CONV_CH:CONV_CH + ATTN_CH]], axis=1)

    xf = x.reshape(n, D_MODEL)
    for l in range(depth):
        ws_cat = jnp.transpose(jnp.tril(gmlp_w_s[l]), (1, 0, 2)).reshape(CHUNK, GMLP_GROUPS * CHUNK)
        bs_tab = jnp.repeat(gmlp_b_s[l].T, GMLP_GROUP_DIM, axis=1)
        qkv, ac = _inproj(xf, seq, (w_in_b, l), row2d(b_in[l]), conv_dw_w[l],
                          row2d(conv_dw_b[l]), row2d(conv_ln_g[l]), row2d(conv_ln_b[l]),
                          row2d(gmlp_ln_g[l]), row2d(gmlp_ln_b[l]), ws_cat.astype(BF16), bs_tab)
        attn = _attention(qkv, batch, seq, bias)
        xf = _post(alpha, ac, attn, xf, seq, (w_out_b, l), row2d(b_out[l]), row2d(ln1_g[l]),
                   row2d(ln1_b[l]), (w_up_b, l), row2d(ffn_b_up[l]), (ffn_conv_w, l),
                   row2d(ffn_conv_b[l] + ffn_b_up[l] * jnp.sum(ffn_conv_w[l], axis=0)),
                   (w_down_b, l), row2d(ffn_b_down[l]), row2d(ln2_g[l]), row2d(ln2_b[l]))
    return xf.reshape(batch, seq, D_MODEL)
```

```python
import functools
import math

import jax
import jax.numpy as jnp
import numpy as np
from jax import lax
from jax.experimental import pallas as pl
from jax.experimental.pallas import tpu as pltpu

D_MODEL = 1024
HEAD_DIM = 64
CONV_CH = 256
CONV_WIDTH = 31
ATTN_HEADS = 8
ATTN_CH = ATTN_HEADS * HEAD_DIM
DILATED_PATTERNS = ((128, 1), (512, 4), (2048, 16))
ATTN_BLOCK = 128
N_BUCKETS = 32
MAX_DISTANCE = 2048
GMLP_CH = 256
GMLP_GROUPS = 4
GMLP_GROUP_DIM = GMLP_CH // GMLP_GROUPS
CHUNK = 128
MIX_CH = CONV_CH + ATTN_CH + GMLP_CH
IN_CH = 2 * CONV_CH + 3 * ATTN_CH + 2 * GMLP_CH
D_FF = 2816
FFN_CONV_WIDTH = 3
LN_EPS = 1e-5

LANES = 128
COL_BLOCK = 512
COL_A, COL_Q, COL_K, COL_V, COL_C = range(IN_CH // COL_BLOCK)
QKV_Q, QKV_K, QKV_V = range(3)

ROW_TILE = 512
CONV_HALO = 32
CONV_ROW_CHUNK = 64
HEAD_PAIR = 2 * HEAD_DIM
N_PAIRS = ATTN_HEADS // 2
ATTN_TILE = ATTN_BLOCK * max(d for _, d in DILATED_PATTERNS)
ATTN_AHEAD = 1
ATTN_GROUP = 1
FF_CHUNK = 256
FFN_LOOKAHEAD = 1
FFN_SLOTS = FFN_LOOKAHEAD + 1
FFN_ROWS = 64
POST_TILES = 1
FFN_HALO = 8
LOG2_E = math.log2(math.e)
NEG_LOGIT = -1e30
V7X_VMEM_BYTES = 64 * 1024 * 1024
VMEM_LIMIT_BYTES = V7X_VMEM_BYTES - 8 * 1024 * 1024

BF16 = jnp.bfloat16
F32 = jnp.float32


def _const_spec(shape):
    zeros = (0,) * len(shape)
    return pl.BlockSpec(shape, lambda *_: zeros, pipeline_mode=pl.Buffered(1))


def _const_operand(c):
    if not isinstance(c, tuple):
        return c, _const_spec(c.shape)
    stacked, layer = c
    index = (layer,) + (0,) * (stacked.ndim - 1)
    return stacked, pl.BlockSpec((None,) + stacked.shape[1:], lambda *_: index,
                                 pipeline_mode=pl.Buffered(1))


def _params(*semantics):
    return pltpu.CompilerParams(dimension_semantics=semantics,
                                vmem_limit_bytes=VMEM_LIMIT_BYTES)


def _layernorm(x, g, b):
    mu = jnp.mean(x, axis=-1, keepdims=True)
    xc = x - mu
    var = jnp.mean(xc * xc, axis=-1, keepdims=True)
    return xc * lax.rsqrt(var + LN_EPS) * g + b


def _sigmoid(x):
    return 0.5 * jnp.tanh(0.5 * x) + 0.5


def _inproj_body(tiles_per_seq, x_ref, w_ref, b_ref, dww_ref, dwb_ref, cg_ref, cb_ref,
                 gg_ref, gb_ref, ws_ref, bs_ref, qkv_ref, o_ref, buf_ref, c_ref):
    first = (pl.program_id(0) % tiles_per_seq) == 0
    xb = x_ref[...].astype(BF16)

    def project(col):
        cols = slice(col * COL_BLOCK, (col + 1) * COL_BLOCK)
        return jnp.dot(xb, w_ref[:, cols], preferred_element_type=F32) + b_ref[:, cols]

    a = project(COL_A)
    glu = a[:, :CONV_CH] * _sigmoid(a[:, CONV_CH:])
    slabs = CONV_CH // LANES
    for s in range(slabs):
        lanes = slice(s * LANES, (s + 1) * LANES)
        prev_tail = buf_ref[s, pl.ds(2 * ROW_TILE, CONV_HALO, stride=2), :]
        buf_ref[s, pl.ds(0, CONV_HALO, stride=2), :] = jnp.where(first, 0.0, prev_tail)
        buf_ref[s, pl.ds(2 * CONV_HALO, ROW_TILE, stride=2), :] = glu[:, lanes]
    lead = CONV_HALO - (CONV_WIDTH - 1)

    def conv_rows(r0):
        accs = []
        for s in range(slabs):
            lanes = slice(s * LANES, (s + 1) * LANES)
            acc = jnp.broadcast_to(dwb_ref[:, lanes], (CONV_ROW_CHUNK, LANES))
            for k in range(CONV_WIDTH):
                window = pl.ds(2 * (r0 + lead + k), CONV_ROW_CHUNK, stride=2)
                acc = acc + dww_ref[k:k + 1, lanes] * buf_ref[s, window, :]
            accs.append(acc)
        y = _layernorm(jnp.concatenate(accs, axis=1), cg_ref[...], cb_ref[...])
        o_ref[r0:r0 + CONV_ROW_CHUNK, 0:CONV_CH] = (y * _sigmoid(y)).astype(BF16)

    c_ref[...] = project(COL_C)
    for j, col in enumerate((COL_Q, COL_K, COL_V)):
        qkv_ref[:, j * COL_BLOCK:(j + 1) * COL_BLOCK] = project(col)
    for r0 in range(0, ROW_TILE, CONV_ROW_CHUNK):
        conv_rows(r0)

    lane = lax.broadcasted_iota(jnp.int32, (1, GMLP_CH), 1)
    for ch in range(ROW_TILE // CHUNK):
        rows = slice(ch * CHUNK, (ch + 1) * CHUNK)
        c = c_ref[rows, :]
        vn = _layernorm(c[:, GMLP_CH:], gg_ref[...], gb_ref[...])
        stacked = jnp.concatenate(
            [jnp.where((lane // GMLP_GROUP_DIM) == g, vn, 0.0) for g in range(GMLP_GROUPS)],
            axis=0).astype(BF16)
        mixed = jnp.dot(ws_ref[...], stacked, preferred_element_type=F32) + bs_ref[...]
        o_ref[rows, CONV_CH:] = (c[:, :GMLP_CH] * mixed).astype(BF16)


def _inproj(x, seq, w, b, dww, dwb, cg, cb, gg, gb, ws_cat, bs_tab):
    n = x.shape[0]
    consts, const_specs = zip(*map(_const_operand, [w, b, dww, dwb, cg, cb, gg, gb, ws_cat, bs_tab]))
    row = lambda width: pl.BlockSpec((ROW_TILE, width), lambda i: (i, 0))
    return pl.pallas_call(
        functools.partial(_inproj_body, seq // ROW_TILE),
        grid=(n // ROW_TILE,),
        in_specs=[row(D_MODEL)] + list(const_specs),
        out_specs=[row(3 * ATTN_CH), row(CONV_CH + GMLP_CH)],
        out_shape=[jax.ShapeDtypeStruct((n, 3 * ATTN_CH), F32),
                   jax.ShapeDtypeStruct((n, CONV_CH + GMLP_CH), BF16)],
        scratch_shapes=[pltpu.VMEM((CONV_CH // LANES, 2 * (CONV_HALO + ROW_TILE), LANES), F32),
                        pltpu.VMEM((ROW_TILE, 2 * GMLP_CH), F32)],
        compiler_params=_params("arbitrary"),
        name="inproj_conv_gmlp",
    )(x, *consts)


def _attn_scores(q2, k2, bias_lo, bias_hi):
    lo = lax.broadcasted_iota(jnp.int32, (1, HEAD_PAIR), 1) < HEAD_DIM
    nt = (((1,), (1,)), ((), ()))
    q2 = q2 * (HEAD_DIM ** -0.5 * LOG2_E)
    kb = k2.astype(BF16)
    ps, ms = [], []
    for keep, bias in ((lo, bias_lo), (jnp.logical_not(lo), bias_hi)):
        qh = jnp.where(keep, q2, 0.0).astype(BF16)
        s = lax.dot_general(qh, kb, nt, preferred_element_type=F32) + bias
        m = jnp.max(s, axis=-1, keepdims=True)
        ps.append(jnp.exp2(s - m).astype(BF16))
        ms.append(m)
    return jnp.concatenate(ps, axis=1), jnp.where(lo, ms[0], ms[1])


def _attn_values(p, v2):
    lo = lax.broadcasted_iota(jnp.int32, (1, HEAD_PAIR), 1) < HEAD_DIM
    shape = (2 * ATTN_BLOCK, HEAD_PAIR)
    ind_lo = jnp.broadcast_to(jnp.where(lo, 1.0, 0.0), shape).astype(BF16)
    ind_hi = jnp.broadcast_to(jnp.where(lo, 0.0, 1.0), shape).astype(BF16)
    vb = v2.astype(BF16)
    zero = jnp.zeros_like(vb)
    v_stack = jnp.concatenate(
        [jnp.concatenate([jnp.where(lo, vb, zero), ind_lo], axis=1),
         jnp.concatenate([jnp.where(lo, zero, vb), ind_hi], axis=1)], axis=0)
    ol = jnp.dot(p, v_stack, preferred_element_type=F32)
    return ol[:, :HEAD_PAIR], ol[:, HEAD_PAIR:]


def _attn_body(q_ref, kc_ref, kp_ref, vc_ref, vp_ref, bias_ref, out_ref,
               p_scr, o_scr, m_scr, l_scr):
    pair = pl.program_id(1)
    edge = jnp.where(pl.program_id(2) == 0, 0, 1)

    def rows(start, n, d):
        return pl.ds(start, n) if d == 1 else pl.ds(start, n, stride=d)

    def window(cur_ref, prev_ref, d, start, span, seq_edge):
        if seq_edge:
            return jnp.concatenate(
                [prev_ref[rows(ATTN_TILE + start - span, ATTN_BLOCK, d), :],
                 cur_ref[rows(start, ATTN_BLOCK, d), :]], axis=0)
        return cur_ref[rows(start - span, 2 * ATTN_BLOCK, d), :]

    order = sorted(range(len(DILATED_PATTERNS)), key=lambda p: -DILATED_PATTERNS[p][1])
    assert DILATED_PATTERNS[order[-1]][1] == 1
    units = []
    for p in order:
        d = DILATED_PATTERNS[p][1]
        span = d * ATTN_BLOCK
        for r in range(d):
            for j in range(ATTN_TILE // span):
                units.append((p, d, r + j * span, span, j == 0))
    groups = [units[g:g + ATTN_GROUP] for g in range(0, len(units), ATTN_GROUP)]

    def merge(start):
        r = slice(start, start + ATTN_BLOCK)
        ms = [m_scr[p, r, :] for p in range(len(DILATED_PATTERNS))]
        big = functools.reduce(jnp.maximum, ms)
        ws = [jnp.exp2(m - big) for m in ms]
        num = sum(w * o_scr[p, r, :] for p, w in enumerate(ws))
        den = sum(w * l_scr[p, r, :] for p, w in enumerate(ws))
        out_ref[r, :] = (num / den).astype(BF16)

    def scores(group, slot):
        for t, (p, d, start, span, seq_edge) in enumerate(group):
            sel = edge if seq_edge else 1
            probs, m = _attn_scores(q_ref[rows(start, ATTN_BLOCK, d), :],
                                    window(kc_ref, kp_ref, d, start, span, seq_edge),
                                    bias_ref[p, sel, 2 * pair], bias_ref[p, sel, 2 * pair + 1])
            p_scr[slot, t] = probs
            m_scr[p, rows(start, ATTN_BLOCK, d), :] = m

    def values(group, slot):
        for t, (p, d, start, span, seq_edge) in enumerate(group):
            o, l = _attn_values(p_scr[slot, t], window(vc_ref, vp_ref, d, start, span, seq_edge))
            o_scr[p, rows(start, ATTN_BLOCK, d), :] = o
            l_scr[p, rows(start, ATTN_BLOCK, d), :] = l
            if d == 1:
                merge(start)

    slots = ATTN_AHEAD + 1
    for g in range(ATTN_AHEAD):
        scores(groups[g], g % slots)
    for g in range(len(groups)):
        if g + ATTN_AHEAD < len(groups):
            scores(groups[g + ATTN_AHEAD], (g + ATTN_AHEAD) % slots)
        values(groups[g], g % slots)


def _attention(qkv, batch, seq, bias):
    n = qkv.shape[0]
    tiles = seq // ATTN_TILE
    lane_blocks = COL_BLOCK // LANES

    def cur(col):
        return pl.BlockSpec((ATTN_TILE, LANES),
                            lambda b, hp, t: (b * tiles + t, col * lane_blocks + hp))

    def prev(col):
        return pl.BlockSpec((ATTN_TILE, LANES),
                            lambda b, hp, t: (b * tiles + jnp.maximum(t - 1, 0),
                                              col * lane_blocks + hp))

    return pl.pallas_call(
        _attn_body,
        grid=(batch, N_PAIRS, tiles),
        in_specs=[cur(QKV_Q), cur(QKV_K), prev(QKV_K), cur(QKV_V), prev(QKV_V),
                  _const_spec(bias.shape)],
        out_specs=pl.BlockSpec((ATTN_TILE, LANES), lambda b, hp, t: (b * tiles + t, hp)),
        out_shape=jax.ShapeDtypeStruct((n, ATTN_CH), BF16),
        scratch_shapes=[pltpu.VMEM((ATTN_AHEAD + 1, ATTN_GROUP, ATTN_BLOCK, 4 * ATTN_BLOCK), BF16),
                        pltpu.VMEM((len(DILATED_PATTERNS), ATTN_TILE, LANES), F32),
                        pltpu.VMEM((len(DILATED_PATTERNS), ATTN_TILE, LANES), F32),
                        pltpu.VMEM((len(DILATED_PATTERNS), ATTN_TILE, LANES), F32)],
        compiler_params=_params("parallel", "parallel", "arbitrary"),
        name="dilated_attn",
    )(qkv, qkv, qkv, qkv, qkv, bias)


def _t5_bucket(dist):
    max_exact = N_BUCKETS // 2
    d = np.maximum(dist, 1).astype(np.float64)
    large = max_exact + (np.log(d / max_exact) / math.log(MAX_DISTANCE / max_exact)
                         * (N_BUCKETS - max_exact)).astype(np.int32)
    large = np.minimum(large, N_BUCKETS - 1)
    return np.where(dist < max_exact, dist, large).astype(np.int32)


def _attn_bias(rel_table, window, dilation):
    n_win = window // dilation
    assert n_win <= ATTN_BLOCK
    heads = rel_table.shape[1]
    onehot = np.eye(N_BUCKETS, dtype=np.float32)[_t5_bucket(np.arange(n_win + 1) * dilation)]
    by_dist = jnp.sum(onehot[:, :, None] * rel_table.astype(F32)[None], axis=1)
    period = 3 * ATTN_BLOCK + 1
    u = jnp.concatenate([jnp.full((heads, ATTN_BLOCK - n_win), NEG_LOGIT, F32), by_dist[::-1].T,
                         jnp.full((heads, period - ATTN_BLOCK - 1), NEG_LOGIT, F32)], axis=1)
    toeplitz = jnp.tile(u, (1, ATTN_BLOCK))[:, :ATTN_BLOCK * (period - 1)]
    main = toeplitz.reshape(heads, ATTN_BLOCK, period - 1)[:, :, :2 * ATTN_BLOCK]
    kj = np.arange(2 * ATTN_BLOCK)[None, None, :]
    first = jnp.where(jnp.asarray(kj >= ATTN_BLOCK), main, NEG_LOGIT)
    return jnp.stack([first, main]) * LOG2_E


def _post_body(alpha, tiles_per_seq, ac_ref, attn_ref, x_ref, wo_ref, bo_ref, g1_ref, b1_ref,
               wup_ref, bup_ref, cw_ref, cb_ref, wdn_ref, bdn_ref, g2_ref, b2_ref, y_ref,
               x1_ref, xb_ref, h_buf, act_ref, tail_ref):
    mix_half = CONV_CH + GMLP_CH
    for t in range(POST_TILES):
        rows = slice(t * ROW_TILE, (t + 1) * ROW_TILE)
        mix = (jnp.dot(ac_ref[rows, :], wo_ref[0:mix_half, :], preferred_element_type=F32)
               + jnp.dot(attn_ref[rows, :], wo_ref[mix_half:, :], preferred_element_type=F32)
               + bo_ref[...])
        x1 = _layernorm(alpha * x_ref[rows, :] + mix, g1_ref[...], b1_ref[...])
        x1_ref[t] = x1
        xb_ref[t] = x1.astype(BF16)

    tile0 = pl.program_id(0) * POST_TILES
    firsts = [((tile0 + t) % tiles_per_seq) == 0 for t in range(POST_TILES)]
    half_slabs = FF_CHUNK // LANES
    slabs = 2 * half_slabs
    n_chunks = D_FF // FF_CHUNK

    def chunk_cols(c, half):
        start = half * D_FF + c * FF_CHUNK
        return slice(start, start + FF_CHUNK)

    steps = [(t, c) for t in range(POST_TILES) for c in range(n_chunks)]

    def up(n):
        t, c = steps[n]
        for half in range(2):
            cols = chunk_cols(c, half)
            h = jnp.dot(xb_ref[t], wup_ref[:, cols], preferred_element_type=F32)
            halo = jnp.where(firsts[t], -bup_ref[:, cols], tail_ref[:, cols])
            tail_ref[:, cols] = h[ROW_TILE - FFN_HALO:, :]
            for s in range(half_slabs):
                lanes = slice(s * LANES, (s + 1) * LANES)
                buf = h_buf.at[n % FFN_SLOTS, half * half_slabs + s]
                buf[pl.ds(0, FFN_HALO, stride=2), :] = halo[:, lanes]
                buf[pl.ds(2 * FFN_HALO, ROW_TILE, stride=2), :] = h[:, lanes]

    def conv(n, s, r0):
        start = chunk_cols(steps[n][1], s // half_slabs).start + (s % half_slabs) * LANES
        wcols = slice(start, start + LANES)
        y = cb_ref[:, wcols]
        for k in range(FFN_CONV_WIDTH):
            shift = FFN_CONV_WIDTH - 1 - k
            y = y + cw_ref[k:k + 1, wcols] * h_buf[n % FFN_SLOTS, s, pl.ds(
                2 * (FFN_HALO + r0 - shift), FFN_ROWS, stride=2), :]
        return y

    for n in range(FFN_LOOKAHEAD):
        up(n)
    acc = None
    for n, (t, c) in enumerate(steps):
        if n + FFN_LOOKAHEAD < len(steps):
            up(n + FFN_LOOKAHEAD)
        for r0 in range(0, ROW_TILE, FFN_ROWS):
            for s in range(half_slabs):
                gate, val = conv(n, s, r0), conv(n, s + half_slabs, r0)
                act_ref[n % FFN_SLOTS, r0:r0 + FFN_ROWS, s * LANES:(s + 1) * LANES] = (
                    gate * _sigmoid(gate) * val).astype(BF16)
        if c == 0:
            acc = alpha * x1_ref[t] + bdn_ref[...]
        acc = acc + jnp.dot(act_ref[n % FFN_SLOTS], wdn_ref[c * FF_CHUNK:(c + 1) * FF_CHUNK, :],
                            preferred_element_type=F32)
        if c == n_chunks - 1:
            y_ref[t * ROW_TILE:(t + 1) * ROW_TILE, :] = _layernorm(acc, g2_ref[...], b2_ref[...])


def _post(alpha, ac, attn, x, seq, wo, bo, g1, b1, wup, bup, cw, cb, wdn, bdn, g2, b2):
    n = x.shape[0]
    rows = POST_TILES * ROW_TILE
    row = lambda width: pl.BlockSpec((rows, width), lambda i: (i, 0))
    consts, const_specs = zip(*map(_const_operand,
                                   [wo, bo, g1, b1, wup, bup, cw, cb, wdn, bdn, g2, b2]))
    return pl.pallas_call(
        functools.partial(_post_body, alpha, seq // ROW_TILE),
        grid=(n // rows,),
        in_specs=[row(CONV_CH + GMLP_CH), row(ATTN_CH), row(D_MODEL)] + list(const_specs),
        out_specs=row(D_MODEL),
        out_shape=jax.ShapeDtypeStruct((n, D_MODEL), F32),
        scratch_shapes=[pltpu.VMEM((POST_TILES, ROW_TILE, D_MODEL), F32),
                        pltpu.VMEM((POST_TILES, ROW_TILE, D_MODEL), BF16),
                        pltpu.VMEM((FFN_SLOTS, 2 * FF_CHUNK // LANES, 2 * (FFN_HALO + ROW_TILE),
                                    LANES), F32),
                        pltpu.VMEM((FFN_SLOTS, ROW_TILE, FF_CHUNK), BF16),
                        pltpu.VMEM((FFN_HALO, 2 * D_FF), F32)],
        compiler_params=_params("arbitrary"),
        name="outproj_convffn",
    )(ac, attn, x, *consts)


def kernel(x, w_in, b_in, conv_dw_w, conv_dw_b, conv_ln_g, conv_ln_b, rel_bias_table, gmlp_ln_g, gmlp_ln_b, gmlp_w_s, gmlp_b_s, w_out, b_out, ln1_g, ln1_b, ffn_w_up, ffn_b_up, ffn_conv_w, ffn_conv_b, ffn_w_down, ffn_b_down, ln2_g, ln2_b):
    batch, seq, d_model = x.shape
    depth = w_in.shape[0]
    assert x.dtype == F32 and d_model == D_MODEL
    assert w_in.shape == (depth, D_MODEL, IN_CH) and w_out.shape == (depth, MIX_CH, D_MODEL)
    assert ffn_w_up.shape == (depth, D_MODEL, 2 * D_FF) and ffn_w_down.shape == (depth, D_FF, D_MODEL)
    assert conv_dw_w.shape == (depth, CONV_WIDTH, CONV_CH) and CONV_WIDTH - 1 <= CONV_HALO
    assert ffn_conv_w.shape == (depth, FFN_CONV_WIDTH, 2 * D_FF) and FFN_CONV_WIDTH - 1 <= FFN_HALO
    assert gmlp_w_s.shape == (depth, GMLP_GROUPS, CHUNK, CHUNK)
    assert rel_bias_table.shape == (N_BUCKETS, ATTN_HEADS)
    assert seq % ATTN_TILE == 0 and seq % (POST_TILES * ROW_TILE) == 0 and ROW_TILE % CHUNK == 0
    assert D_FF % FF_CHUNK == 0 and FF_CHUNK % LANES == 0
    alpha = (2.0 * depth) ** 0.25
    n = batch * seq
    row2d = lambda v: v.reshape(1, -1)
    bias = jnp.stack([_attn_bias(rel_bias_table, w, d) for (w, d) in DILATED_PATTERNS])
    w_in_b, w_up_b, w_down_b = (w.astype(BF16) for w in (w_in, ffn_w_up, ffn_w_down))
    w_out_b = w_out.astype(BF16)
    w_out_b = jnp.concatenate([w_out_b[:, :CONV_CH], w_out_b[:, CONV_CH + ATTN_CH:],
                               w_out_b[:, CONV_CH:CONV_CH + ATTN_CH]], axis=1)

    xf = x.reshape(n, D_MODEL)
    for l in range(depth):
        ws_cat = jnp.transpose(jnp.tril(gmlp_w_s[l]), (1, 0, 2)).reshape(CHUNK, GMLP_GROUPS * CHUNK)
        bs_tab = jnp.repeat(gmlp_b_s[l].T, GMLP_GROUP_DIM, axis=1)
        qkv, ac = _inproj(xf, seq, (w_in_b, l), row2d(b_in[l]), conv_dw_w[l],
                          row2d(conv_dw_b[l]), row2d(conv_ln_g[l]), row2d(conv_ln_b[l]),
                          row2d(gmlp_ln_g[l]), row2d(gmlp_ln_b[l]), ws_cat.astype(BF16), bs_tab)
        attn = _attention(qkv, batch, seq, bias)
        xf = _post(alpha, ac, attn, xf, seq, (w_out_b, l), row2d(b_out[l]), row2d(ln1_g[l]),
                   row2d(ln1_b[l]), (w_up_b, l), row2d(ffn_b_up[l]), (ffn_conv_w, l),
                   row2d(ffn_conv_b[l] + ffn_b_up[l] * jnp.sum(ffn_conv_w[l], axis=0)),
                   (w_down_b, l), row2d(ffn_b_down[l]), row2d(ln2_g[l]), row2d(ln2_b[l]))
    return xf.reshape(batch, seq, D_MODEL)
```

```python
import functools
import math

import jax
import jax.numpy as jnp
import numpy as np
from jax import lax
from jax.experimental import pallas as pl
from jax.experimental.pallas import tpu as pltpu

D_MODEL = 1024
HEAD_DIM = 64
CONV_CH = 256
CONV_WIDTH = 31
ATTN_HEADS = 8
ATTN_CH = ATTN_HEADS * HEAD_DIM
DILATED_PATTERNS = ((128, 1), (512, 4), (2048, 16))
ATTN_BLOCK = 128
N_BUCKETS = 32
MAX_DISTANCE = 2048
GMLP_CH = 256
GMLP_GROUPS = 4
GMLP_GROUP_DIM = GMLP_CH // GMLP_GROUPS
CHUNK = 128
MIX_CH = CONV_CH + ATTN_CH + GMLP_CH
IN_CH = 2 * CONV_CH + 3 * ATTN_CH + 2 * GMLP_CH
D_FF = 2816
FFN_CONV_WIDTH = 3
LN_EPS = 1e-5

LANES = 128
COL_BLOCK = 512
COL_A, COL_Q, COL_K, COL_V, COL_C = range(IN_CH // COL_BLOCK)
QKV_Q, QKV_K, QKV_V = range(3)

ROW_TILE = 512
CONV_HALO = 32
CONV_ROW_CHUNK = 64
HEAD_PAIR = 2 * HEAD_DIM
N_PAIRS = ATTN_HEADS // 2
ATTN_TILE = ATTN_BLOCK * max(d for _, d in DILATED_PATTERNS)
ATTN_AHEAD = 1
ATTN_GROUP = 1
FF_CHUNK = 256
FFN_LOOKAHEAD = 2
FFN_SLOTS = FFN_LOOKAHEAD + 1
FFN_ROWS = 64
POST_TILES = 1
FFN_HALO = 8
LOG2_E = math.log2(math.e)
NEG_LOGIT = -1e30
V7X_VMEM_BYTES = 64 * 1024 * 1024
VMEM_LIMIT_BYTES = V7X_VMEM_BYTES - 8 * 1024 * 1024

BF16 = jnp.bfloat16
F32 = jnp.float32


def _const_spec(shape):
    zeros = (0,) * len(shape)
    return pl.BlockSpec(shape, lambda *_: zeros, pipeline_mode=pl.Buffered(1))


def _const_operand(c):
    if not isinstance(c, tuple):
        return c, _const_spec(c.shape)
    stacked, layer = c
    index = (layer,) + (0,) * (stacked.ndim - 1)
    return stacked, pl.BlockSpec((None,) + stacked.shape[1:], lambda *_: index,
                                 pipeline_mode=pl.Buffered(1))


def _params(*semantics):
    return pltpu.CompilerParams(dimension_semantics=semantics,
                                vmem_limit_bytes=VMEM_LIMIT_BYTES)


def _layernorm(x, g, b):
    mu = jnp.mean(x, axis=-1, keepdims=True)
    xc = x - mu
    var = jnp.mean(xc * xc, axis=-1, keepdims=True)
    return xc * lax.rsqrt(var + LN_EPS) * g + b


def _sigmoid(x):
    return 0.5 * jnp.tanh(0.5 * x) + 0.5


def _inproj_body(tiles_per_seq, x_ref, w_ref, b_ref, dww_ref, dwb_ref, cg_ref, cb_ref,
                 gg_ref, gb_ref, ws_ref, bs_ref, qkv_ref, o_ref, xb_ref, buf_ref, c_ref):
    first = (pl.program_id(0) % tiles_per_seq) == 0
    xb_ref[...] = x_ref[...].astype(BF16)

    def project(col):
        cols = slice(col * COL_BLOCK, (col + 1) * COL_BLOCK)
        return jnp.dot(xb_ref[...], w_ref[:, cols], preferred_element_type=F32) + b_ref[:, cols]

    a = project(COL_A)
    glu = a[:, :CONV_CH] * _sigmoid(a[:, CONV_CH:])
    slabs = CONV_CH // LANES
    for s in range(slabs):
        lanes = slice(s * LANES, (s + 1) * LANES)
        prev_tail = buf_ref[s, pl.ds(2 * ROW_TILE, CONV_HALO, stride=2), :]
        buf_ref[s, pl.ds(0, CONV_HALO, stride=2), :] = jnp.where(first, 0.0, prev_tail)
        buf_ref[s, pl.ds(2 * CONV_HALO, ROW_TILE, stride=2), :] = glu[:, lanes]
    lead = CONV_HALO - (CONV_WIDTH - 1)

    def conv_rows(r0):
        accs = []
        for s in range(slabs):
            lanes = slice(s * LANES, (s + 1) * LANES)
            acc = jnp.broadcast_to(dwb_ref[:, lanes], (CONV_ROW_CHUNK, LANES))
            for k in range(CONV_WIDTH):
                window = pl.ds(2 * (r0 + lead + k), CONV_ROW_CHUNK, stride=2)
                acc = acc + dww_ref[k:k + 1, lanes] * buf_ref[s, window, :]
            accs.append(acc)
        y = _layernorm(jnp.concatenate(accs, axis=1), cg_ref[...], cb_ref[...])
        o_ref[r0:r0 + CONV_ROW_CHUNK, 0:CONV_CH] = (y * _sigmoid(y)).astype(BF16)

    c_ref[...] = project(COL_C)
    for j, col in enumerate((COL_Q, COL_K, COL_V)):
        qkv_ref[:, j * COL_BLOCK:(j + 1) * COL_BLOCK] = project(col)
    for r0 in range(0, ROW_TILE, CONV_ROW_CHUNK):
        conv_rows(r0)

    lane = lax.broadcasted_iota(jnp.int32, (1, GMLP_CH), 1)
    for ch in range(ROW_TILE // CHUNK):
        rows = slice(ch * CHUNK, (ch + 1) * CHUNK)
        c = c_ref[rows, :]
        vn = _layernorm(c[:, GMLP_CH:], gg_ref[...], gb_ref[...])
        stacked = jnp.concatenate(
            [jnp.where((lane // GMLP_GROUP_DIM) == g, vn, 0.0) for g in range(GMLP_GROUPS)],
            axis=0).astype(BF16)
        mixed = jnp.dot(ws_ref[...], stacked, preferred_element_type=F32) + bs_ref[...]
        o_ref[rows, CONV_CH:] = (c[:, :GMLP_CH] * mixed).astype(BF16)


def _inproj(x, seq, w, b, dww, dwb, cg, cb, gg, gb, ws_cat, bs_tab):
    n = x.shape[0]
    consts, const_specs = zip(*map(_const_operand, [w, b, dww, dwb, cg, cb, gg, gb, ws_cat, bs_tab]))
    row = lambda width: pl.BlockSpec((ROW_TILE, width), lambda i: (i, 0))
    return pl.pallas_call(
        functools.partial(_inproj_body, seq // ROW_TILE),
        grid=(n // ROW_TILE,),
        in_specs=[row(D_MODEL)] + list(const_specs),
        out_specs=[row(3 * ATTN_CH), row(CONV_CH + GMLP_CH)],
        out_shape=[jax.ShapeDtypeStruct((n, 3 * ATTN_CH), F32),
                   jax.ShapeDtypeStruct((n, CONV_CH + GMLP_CH), BF16)],
        scratch_shapes=[pltpu.VMEM((ROW_TILE, D_MODEL), BF16),
                        pltpu.VMEM((CONV_CH // LANES, 2 * (CONV_HALO + ROW_TILE), LANES), F32),
                        pltpu.VMEM((ROW_TILE, 2 * GMLP_CH), F32)],
        compiler_params=_params("arbitrary"),
        name="inproj_conv_gmlp",
    )(x, *consts)


def _attn_scores(q2, k2, bias_lo, bias_hi):
    lo = lax.broadcasted_iota(jnp.int32, (1, HEAD_PAIR), 1) < HEAD_DIM
    nt = (((1,), (1,)), ((), ()))
    q2 = q2 * (HEAD_DIM ** -0.5 * LOG2_E)
    kb = k2.astype(BF16)
    ps, ms = [], []
    for keep, bias in ((lo, bias_lo), (jnp.logical_not(lo), bias_hi)):
        qh = jnp.where(keep, q2, 0.0).astype(BF16)
        s = lax.dot_general(qh, kb, nt, preferred_element_type=F32) + bias
        m = jnp.max(s, axis=-1, keepdims=True)
        ps.append(jnp.exp2(s - m).astype(BF16))
        ms.append(m)
    return jnp.concatenate(ps, axis=1), jnp.where(lo, ms[0], ms[1])


def _attn_values(p, v2):
    lo = lax.broadcasted_iota(jnp.int32, (1, HEAD_PAIR), 1) < HEAD_DIM
    shape = (2 * ATTN_BLOCK, HEAD_PAIR)
    ind_lo = jnp.broadcast_to(jnp.where(lo, 1.0, 0.0), shape).astype(BF16)
    ind_hi = jnp.broadcast_to(jnp.where(lo, 0.0, 1.0), shape).astype(BF16)
    vb = v2.astype(BF16)
    zero = jnp.zeros_like(vb)
    v_stack = jnp.concatenate(
        [jnp.concatenate([jnp.where(lo, vb, zero), ind_lo], axis=1),
         jnp.concatenate([jnp.where(lo, zero, vb), ind_hi], axis=1)], axis=0)
    ol = jnp.dot(p, v_stack, preferred_element_type=F32)
    return ol[:, :HEAD_PAIR], ol[:, HEAD_PAIR:]


def _attn_body(q_ref, kc_ref, kp_ref, vc_ref, vp_ref, bias_ref, out_ref,
               p_scr, o_scr, m_scr, l_scr):
    pair = pl.program_id(1)
    edge = jnp.where(pl.program_id(2) == 0, 0, 1)

    def rows(start, n, d):
        return pl.ds(start, n) if d == 1 else pl.ds(start, n, stride=d)

    def window(cur_ref, prev_ref, d, start, span, seq_edge):
        if seq_edge:
            return jnp.concatenate(
                [prev_ref[rows(ATTN_TILE + start - span, ATTN_BLOCK, d), :],
                 cur_ref[rows(start, ATTN_BLOCK, d), :]], axis=0)
        return cur_ref[rows(start - span, 2 * ATTN_BLOCK, d), :]

    order = sorted(range(len(DILATED_PATTERNS)), key=lambda p: -DILATED_PATTERNS[p][1])
    assert DILATED_PATTERNS[order[-1]][1] == 1
    units = []
    for p in order:
        d = DILATED_PATTERNS[p][1]
        span = d * ATTN_BLOCK
        for r in range(d):
            for j in range(ATTN_TILE // span):
                units.append((p, d, r + j * span, span, j == 0))
    groups = [units[g:g + ATTN_GROUP] for g in range(0, len(units), ATTN_GROUP)]

    def merge(start):
        r = slice(start, start + ATTN_BLOCK)
        ms = [m_scr[p, r, :] for p in range(len(DILATED_PATTERNS))]
        big = functools.reduce(jnp.maximum, ms)
        ws = [jnp.exp2(m - big) for m in ms]
        num = sum(w * o_scr[p, r, :] for p, w in enumerate(ws))
        den = sum(w * l_scr[p, r, :] for p, w in enumerate(ws))
        out_ref[r, :] = (num / den).astype(BF16)

    def scores(group, slot):
        for t, (p, d, start, span, seq_edge) in enumerate(group):
            sel = edge if seq_edge else 1
            probs, m = _attn_scores(q_ref[rows(start, ATTN_BLOCK, d), :],
                                    window(kc_ref, kp_ref, d, start, span, seq_edge),
                                    bias_ref[p, sel, 2 * pair], bias_ref[p, sel, 2 * pair + 1])
            p_scr[slot, t] = probs
            m_scr[p, rows(start, ATTN_BLOCK, d), :] = m

    def values(group, slot):
        for t, (p, d, start, span, seq_edge) in enumerate(group):
            o, l = _attn_values(p_scr[slot, t], window(vc_ref, vp_ref, d, start, span, seq_edge))
            o_scr[p, rows(start, ATTN_BLOCK, d), :] = o
            l_scr[p, rows(start, ATTN_BLOCK, d), :] = l
            if d == 1:
                merge(start)

    slots = ATTN_AHEAD + 1
    for g in range(ATTN_AHEAD):
        scores(groups[g], g % slots)
    for g in range(len(groups)):
        if g + ATTN_AHEAD < len(groups):
            scores(groups[g + ATTN_AHEAD], (g + ATTN_AHEAD) % slots)
        values(groups[g], g % slots)


def _attention(qkv, batch, seq, bias):
    n = qkv.shape[0]
    tiles = seq // ATTN_TILE
    lane_blocks = COL_BLOCK // LANES

    def cur(col):
        return pl.BlockSpec((ATTN_TILE, LANES),
                            lambda b, hp, t: (b * tiles + t, col * lane_blocks + hp))

    def prev(col):
        return pl.BlockSpec((ATTN_TILE, LANES),
                            lambda b, hp, t: (b * tiles + jnp.maximum(t - 1, 0),
                                              col * lane_blocks + hp))

    return pl.pallas_call(
        _attn_body,
        grid=(batch, N_PAIRS, tiles),
        in_specs=[cur(QKV_Q), cur(QKV_K), prev(QKV_K), cur(QKV_V), prev(QKV_V),
                  _const_spec(bias.shape)],
        out_specs=pl.BlockSpec((ATTN_TILE, LANES), lambda b, hp, t: (b * tiles + t, hp)),
        out_shape=jax.ShapeDtypeStruct((n, ATTN_CH), BF16),
        scratch_shapes=[pltpu.VMEM((ATTN_AHEAD + 1, ATTN_GROUP, ATTN_BLOCK, 4 * ATTN_BLOCK), BF16),
                        pltpu.VMEM((len(DILATED_PATTERNS), ATTN_TILE, LANES), F32),
                        pltpu.VMEM((len(DILATED_PATTERNS), ATTN_TILE, LANES), F32),
                        pltpu.VMEM((len(DILATED_PATTERNS), ATTN_TILE, LANES), F32)],
        compiler_params=_params("parallel", "parallel", "arbitrary"),
        name="dilated_attn",
    )(qkv, qkv, qkv, qkv, qkv, bias)


def _t5_bucket(dist):
    max_exact = N_BUCKETS // 2
    d = np.maximum(dist, 1).astype(np.float64)
    large = max_exact + (np.log(d / max_exact) / math.log(MAX_DISTANCE / max_exact)
                         * (N_BUCKETS - max_exact)).astype(np.int32)
    large = np.minimum(large, N_BUCKETS - 1)
    return np.where(dist < max_exact, dist, large).astype(np.int32)


def _attn_bias(rel_table, window, dilation):
    n_win = window // dilation
    assert n_win <= ATTN_BLOCK
    heads = rel_table.shape[1]
    onehot = np.eye(N_BUCKETS, dtype=np.float32)[_t5_bucket(np.arange(n_win + 1) * dilation)]
    by_dist = jnp.sum(onehot[:, :, None] * rel_table.astype(F32)[None], axis=1)
    period = 3 * ATTN_BLOCK + 1
    u = jnp.concatenate([jnp.full((heads, ATTN_BLOCK - n_win), NEG_LOGIT, F32), by_dist[::-1].T,
                         jnp.full((heads, period - ATTN_BLOCK - 1), NEG_LOGIT, F32)], axis=1)
    toeplitz = jnp.tile(u, (1, ATTN_BLOCK))[:, :ATTN_BLOCK * (period - 1)]
    main = toeplitz.reshape(heads, ATTN_BLOCK, period - 1)[:, :, :2 * ATTN_BLOCK]
    kj = np.arange(2 * ATTN_BLOCK)[None, None, :]
    first = jnp.where(jnp.asarray(kj >= ATTN_BLOCK), main, NEG_LOGIT)
    return jnp.stack([first, main]) * LOG2_E


def _post_body(alpha, tiles_per_seq, ac_ref, attn_ref, x_ref, wo_ref, bo_ref, g1_ref, b1_ref,
               wup_ref, bup_ref, cw_ref, cb_ref, wdn_ref, bdn_ref, g2_ref, b2_ref, y_ref,
               x1_ref, xb_ref, h_buf, act_ref, tail_ref):
    mix_half = CONV_CH + GMLP_CH
    for t in range(POST_TILES):
        rows = slice(t * ROW_TILE, (t + 1) * ROW_TILE)
        mix = (jnp.dot(ac_ref[rows, :], wo_ref[0:mix_half, :], preferred_element_type=F32)
               + jnp.dot(attn_ref[rows, :], wo_ref[mix_half:, :], preferred_element_type=F32)
               + bo_ref[...])
        x1 = _layernorm(alpha * x_ref[rows, :] + mix, g1_ref[...], b1_ref[...])
        x1_ref[t] = x1
        xb_ref[t] = x1.astype(BF16)

    tile0 = pl.program_id(0) * POST_TILES
    firsts = [((tile0 + t) % tiles_per_seq) == 0 for t in range(POST_TILES)]
    half_slabs = FF_CHUNK // LANES
    slabs = 2 * half_slabs
    n_chunks = D_FF // FF_CHUNK

    def chunk_cols(c, half):
        start = half * D_FF + c * FF_CHUNK
        return slice(start, start + FF_CHUNK)

    steps = [(t, c) for t in range(POST_TILES) for c in range(n_chunks)]

    def up(n):
        t, c = steps[n]
        for half in range(2):
            cols = chunk_cols(c, half)
            h = jnp.dot(xb_ref[t], wup_ref[:, cols], preferred_element_type=F32)
            halo = jnp.where(firsts[t], -bup_ref[:, cols], tail_ref[:, cols])
            tail_ref[:, cols] = h[ROW_TILE - FFN_HALO:, :]
            for s in range(half_slabs):
                lanes = slice(s * LANES, (s + 1) * LANES)
                buf = h_buf.at[n % FFN_SLOTS, half * half_slabs + s]
                buf[pl.ds(0, FFN_HALO, stride=2), :] = halo[:, lanes]
                buf[pl.ds(2 * FFN_HALO, ROW_TILE, stride=2), :] = h[:, lanes]

    def conv(n, s, r0):
        start = chunk_cols(steps[n][1], s // half_slabs).start + (s % half_slabs) * LANES
        wcols = slice(start, start + LANES)
        y = cb_ref[:, wcols]
        for k in range(FFN_CONV_WIDTH):
            shift = FFN_CONV_WIDTH - 1 - k
            y = y + cw_ref[k:k + 1, wcols] * h_buf[n % FFN_SLOTS, s, pl.ds(
                2 * (FFN_HALO + r0 - shift), FFN_ROWS, stride=2), :]
        return y

    for n in range(FFN_LOOKAHEAD):
        up(n)
    acc = None
    for n, (t, c) in enumerate(steps):
        if n + FFN_LOOKAHEAD < len(steps):
            up(n + FFN_LOOKAHEAD)
        for r0 in range(0, ROW_TILE, FFN_ROWS):
            for s in range(half_slabs):
                gate, val = conv(n, s, r0), conv(n, s + half_slabs, r0)
                act_ref[n % FFN_SLOTS, r0:r0 + FFN_ROWS, s * LANES:(s + 1) * LANES] = (
                    gate * _sigmoid(gate) * val).astype(BF16)
        if c == 0:
            acc = alpha * x1_ref[t] + bdn_ref[...]
        acc = acc + jnp.dot(act_ref[n % FFN_SLOTS], wdn_ref[c * FF_CHUNK:(c + 1) * FF_CHUNK, :],
                            preferred_element_type=F32)
        if c == n_chunks - 1:
            y_ref[t * ROW_TILE:(t + 1) * ROW_TILE, :] = _layernorm(acc, g2_ref[...], b2_ref[...])


def _post(alpha, ac, attn, x, seq, wo, bo, g1, b1, wup, bup, cw, cb, wdn, bdn, g2, b2):
    n = x.shape[0]
    rows = POST_TILES * ROW_TILE
    row = lambda width: pl.BlockSpec((rows, width), lambda i: (i, 0))
    consts, const_specs = zip(*map(_const_operand,
                                   [wo, bo, g1, b1, wup, bup, cw, cb, wdn, bdn, g2, b2]))
    return pl.pallas_call(
        functools.partial(_post_body, alpha, seq // ROW_TILE),
        grid=(n // rows,),
        in_specs=[row(CONV_CH + GMLP_CH), row(ATTN_CH), row(D_MODEL)] + list(const_specs),
        out_specs=row(D_MODEL),
        out_shape=jax.ShapeDtypeStruct((n, D_MODEL), F32),
        scratch_shapes=[pltpu.VMEM((POST_TILES, ROW_TILE, D_MODEL), F32),
                        pltpu.VMEM((POST_TILES, ROW_TILE, D_MODEL), BF16),
                        pltpu.VMEM((FFN_SLOTS, 2 * FF_CHUNK // LANES, 2 * (FFN_HALO + ROW_TILE),
                                    LANES), F32),
                        pltpu.VMEM((FFN_SLOTS, ROW_TILE, FF_CHUNK), BF16),
                        pltpu.VMEM((FFN_HALO, 2 * D_FF), F32)],
        compiler_params=_params("arbitrary"),
        name="outproj_convffn",
    )(ac, attn, x, *consts)


def kernel(x, w_in, b_in, conv_dw_w, conv_dw_b, conv_ln_g, conv_ln_b, rel_bias_table, gmlp_ln_g, gmlp_ln_b, gmlp_w_s, gmlp_b_s, w_out, b_out, ln1_g, ln1_b, ffn_w_up, ffn_b_up, ffn_conv_w, ffn_conv_b, ffn_w_down, ffn_b_down, ln2_g, ln2_b):
    batch, seq, d_model = x.shape
    depth = w_in.shape[0]
    assert x.dtype == F32 and d_model == D_MODEL
    assert w_in.shape == (depth, D_MODEL, IN_CH) and w_out.shape == (depth, MIX_CH, D_MODEL)
    assert ffn_w_up.shape == (depth, D_MODEL, 2 * D_FF) and ffn_w_down.shape == (depth, D_FF, D_MODEL)
    assert conv_dw_w.shape == (depth, CONV_WIDTH, CONV_CH) and CONV_WIDTH - 1 <= CONV_HALO
    assert ffn_conv_w.shape == (depth, FFN_CONV_WIDTH, 2 * D_FF) and FFN_CONV_WIDTH - 1 <= FFN_HALO
    assert gmlp_w_s.shape == (depth, GMLP_GROUPS, CHUNK, CHUNK)
    assert rel_bias_table.shape == (N_BUCKETS, ATTN_HEADS)
    assert seq % ATTN_TILE == 0 and seq % (POST_TILES * ROW_TILE) == 0 and ROW_TILE % CHUNK == 0
    assert D_FF % FF_CHUNK == 0 and FF_CHUNK % LANES == 0
    alpha = (2.0 * depth) ** 0.25
    n = batch * seq
    row2d = lambda v: v.reshape(1, -1)
    bias = jnp.stack([_attn_bias(rel_bias_table, w, d) for (w, d) in DILATED_PATTERNS])
    w_in_b, w_up_b, w_down_b = (w.astype(BF16) for w in (w_in, ffn_w_up, ffn_w_down))
    w_out_b = w_out.astype(BF16)
    w_out_b = jnp.concatenate([w_out_b[:, :CONV_CH], w_out_b[:, CONV_CH + ATTN_CH:],
                               w_out_b[:, CONV_CH:CONV_CH + ATTN_CH]], axis=1)

    xf = x.reshape(n, D_MODEL)
    for l in range(depth):
        ws_cat = jnp.transpose(jnp.tril(gmlp_w_s[l]), (1, 0, 2)).reshape(CHUNK, GMLP_GROUPS * CHUNK)
        bs_tab = jnp.repeat(gmlp_b_s[l].T, GMLP_GROUP_DIM, axis=1)
        qkv, ac = _inproj(xf, seq, (w_in_b, l), row2d(b_in[l]), conv_dw_w[l],
                          row2d(conv_dw_b[l]), row2d(conv_ln_g[l]), row2d(conv_ln_b[l]),
                          row2d(gmlp_ln_g[l]), row2d(gmlp_ln_b[l]), ws_cat.astype(BF16), bs_tab)
        attn = _attention(qkv, batch, seq, bias)
        xf = _post(alpha, ac, attn, xf, seq, (w_out_b, l), row2d(b_out[l]), row2d(ln1_g[l]),
                   row2d(ln1_b[l]), (w_up_b, l), row2d(ffn_b_up[l]), (ffn_conv_w, l),
                   row2d(ffn_conv_b[l] + ffn_b_up[l] * jnp.sum(ffn_conv_w[l], axis=0)),
                   (w_down_b, l), row2d(ffn_b_down[l]), row2d(ln2_g[l]), row2d(ln2_b[l]))
    return xf.reshape(batch, seq, D_MODEL)
```
